```python
import jax, jax.numpy as jnp
from jax import lax
import numpy as np

D_MODEL = 1024
BATCH = 8
SEQ = 8192
DEPTH = 1

D_FF = 2816
MACARON_WEIGHT = 0.5
NORM_EPS = 1e-6
GLA_HEADS = 4
GLA_DK = 64
GLA_DV = 128
GLA_GATE_RANK = 16
GLA_GATE_NORMALIZER = 16.0
GLA_CHUNK = 64
RWKV_HEADS = 8
RWKV_N = 64
RWKV_DECAY_RANK = 64
RWKV_AAA_RANK = 64
RWKV_GATE_RANK = 128
RWKV_GN_EPS = 64e-5
GLA_WIDTH = GLA_HEADS * GLA_DV
RWKV_WIDTH = RWKV_HEADS * RWKV_N
MIX_WIDTH = GLA_WIDTH + RWKV_WIDTH
GLA_COLS = (GLA_HEADS * GLA_DK, GLA_HEADS * GLA_DK, GLA_WIDTH, GLA_WIDTH, GLA_GATE_RANK)
RWKV_COLS = (RWKV_WIDTH, RWKV_WIDTH, RWKV_WIDTH, RWKV_DECAY_RANK, RWKV_AAA_RANK, RWKV_GATE_RANK)
GLA_PROJ = 2 * GLA_HEADS * GLA_DK + 2 * GLA_WIDTH + GLA_GATE_RANK
RWKV_PROJ = 3 * RWKV_WIDTH + RWKV_DECAY_RANK + RWKV_AAA_RANK + RWKV_GATE_RANK
PROJ_WIDTH = GLA_PROJ + RWKV_PROJ

kernel_name = "hybrid_gla_rwkv7_macaron_block"


def _rms_norm(x, g):
    xf = x.astype(jnp.float32)
    y = xf * lax.rsqrt(jnp.mean(xf * xf, axis=-1, keepdims=True) + NORM_EPS)
    return (y * g.astype(jnp.float32)).astype(x.dtype)


def _swiglu(h, w_gate, w_up, w_down):
    return (jax.nn.silu(h @ w_gate) * (h @ w_up)) @ w_down


def _split_cols(t, sizes):
    return jnp.split(t, np.cumsum(sizes)[:-1].tolist(), axis=-1)


def _gla_mixer(q, k, v, g_out, a_lr, alpha_w2, alpha_b, norm_w):
    f32 = jnp.float32
    bsz, seq, _ = q.shape
    n_chunks = seq // GLA_CHUNK
    log_alpha = jax.nn.log_sigmoid((a_lr @ alpha_w2 + alpha_b).astype(f32)) / GLA_GATE_NORMALIZER

    def to_chunks(t, d):
        return t.astype(f32).reshape(bsz, n_chunks, GLA_CHUNK, GLA_HEADS, d).transpose(1, 0, 3, 2, 4)

    qc = to_chunks(q, GLA_DK) * (GLA_DK ** -0.5)
    kc = to_chunks(k, GLA_DK)
    vc = to_chunks(v, GLA_DV)
    gc = to_chunks(log_alpha, GLA_DK)
    causal = jnp.tril(jnp.ones((GLA_CHUNK, GLA_CHUNK), dtype=bool))[:, :, None]

    def chunk_step(state, inp):
        qb, kb, vb, gb = inp
        cum = jnp.cumsum(gb, axis=2)
        o_inter = jnp.einsum('bhcd,bhdv->bhcv', qb * jnp.exp(cum), state)
        rel = jnp.exp(jnp.where(causal, cum[:, :, :, None, :] - cum[:, :, None, :, :], -jnp.inf))
        scores = jnp.einsum('bhid,bhjd,bhijd->bhij', qb, kb, rel)
        o_intra = jnp.einsum('bhij,bhjv->bhiv', scores, vb)
        last = cum[:, :, -1, :]
        new_state = jnp.exp(last)[..., None] * state + jnp.einsum(
            'bhcd,bhcv->bhdv', kb * jnp.exp(last[:, :, None, :] - cum), vb)
        return new_state, o_inter + o_intra

    state0 = jnp.zeros((bsz, GLA_HEADS, GLA_DK, GLA_DV), f32)
    _, oc = lax.scan(chunk_step, state0, (qc, kc, vc, gc))
    o = oc.transpose(1, 0, 3, 2, 4).reshape(bsz, seq, GLA_HEADS, GLA_DV)
    o = o * lax.rsqrt(jnp.mean(o * o, axis=-1, keepdims=True) + NORM_EPS) * norm_w
    o = o * jax.nn.silu(g_out.astype(f32).reshape(bsz, seq, GLA_HEADS, GLA_DV))
    return o.reshape(bsz, seq, GLA_WIDTH)


def _rwkv7_mixer(p, mu, w0, w2, a0, a2, g2, k_k, k_a, r_k, ln_w, ln_b):
    f32 = jnp.float32
    bsz, seq, _ = p.shape
    p = p.astype(f32)
    p_prev = jnp.pad(p[:, :-1], ((0, 0), (1, 0), (0, 0)))
    p = p + mu * (p_prev - p)
    r, k, v, w_lr, a_lr, g_lr = _split_cols(p, RWKV_COLS)
    w = -jax.nn.softplus(-(w0 + jnp.tanh(w_lr) @ w2)) - 0.5
    decay = jnp.exp(-jnp.exp(w))
    a = jax.nn.sigmoid(a0 + a_lr @ a2)
    g = jax.nn.sigmoid(g_lr) @ g2

    def heads(t):
        return t.reshape(bsz, seq, RWKV_HEADS, RWKV_N)

    kk = heads(k * k_k)
    kk = kk / jnp.maximum(jnp.linalg.norm(kk, axis=-1, keepdims=True), 1e-12)
    k = heads(k * (1.0 + (a - 1.0) * k_a))
    a_h = heads(a)
    r, v, decay = heads(r), heads(v), heads(decay)

    def tm(t):
        return t.transpose(1, 0, 2, 3)

    def step(state, inp):
        r_t, w_t, k_t, v_t, a_t, b_t = inp
        sa = jnp.einsum('bhvk,bhk->bhv', state, a_t)
        state = (state * w_t[:, :, None, :] + sa[..., None] * b_t[:, :, None, :]
                 + v_t[..., None] * k_t[:, :, None, :])
        return state, jnp.einsum('bhvk,bhk->bhv', state, r_t)

    state0 = jnp.zeros((bsz, RWKV_HEADS, RWKV_N, RWKV_N), f32)
    _, y = lax.scan(step, state0, (tm(r), tm(decay), tm(k), tm(v), tm(-kk), tm(kk * a_h)))
    y = tm(y)
    mean = jnp.mean(y, axis=-1, keepdims=True)
    var = jnp.mean(jnp.square(y - mean), axis=-1, keepdims=True)
    y = ((y - mean) * lax.rsqrt(var + RWKV_GN_EPS)).reshape(bsz, seq, RWKV_WIDTH) * ln_w + ln_b
    bonus = jnp.sum(r * k * r_k, axis=-1, keepdims=True) * v
    y = y + bonus.reshape(bsz, seq, RWKV_WIDTH)
    return y * g


def setup_inputs(seed: int = 0) -> dict:
    key = jax.random.key(seed)
    ks = jax.random.split(key, 32)
    f32 = jnp.float32
    L = DEPTH

    def nrm(k, shape, scale):
        return jax.random.normal(k, shape, f32) * scale

    def gain(k, shape):
        return 1.0 + 0.02 * jax.random.normal(k, shape, f32)

    return {
        "x": jax.random.normal(ks[0], (BATCH, SEQ, D_MODEL), f32),
        "ffn1_norm": gain(ks[1], (L, D_MODEL)),
        "ffn1_w_gate": nrm(ks[2], (L, D_MODEL, D_FF), D_MODEL ** -0.5),
        "ffn1_w_up": nrm(ks[3], (L, D_MODEL, D_FF), D_MODEL ** -0.5),
        "ffn1_w_down": nrm(ks[4], (L, D_FF, D_MODEL), D_FF ** -0.5),
        "mix_norm": gain(ks[5], (L, D_MODEL)),
        "w_in": nrm(ks[6], (L, D_MODEL, PROJ_WIDTH), D_MODEL ** -0.5),
        "gla_alpha_w2": nrm(ks[7], (L, GLA_GATE_RANK, GLA_HEADS * GLA_DK), GLA_GATE_RANK ** -0.5),
        "gla_alpha_b": nrm(ks[8], (L, GLA_HEADS * GLA_DK), 0.5),
        "gla_norm": gain(ks[9], (L, GLA_DV)),
        "rwkv_mu": jax.random.uniform(ks[10], (L, RWKV_PROJ), f32, 0.0, 1.0),
        "rwkv_w0": jax.random.uniform(ks[11], (L, RWKV_WIDTH), f32, -4.0, 0.0),
        "rwkv_w2": nrm(ks[12], (L, RWKV_DECAY_RANK, RWKV_WIDTH), 0.5 * RWKV_DECAY_RANK ** -0.5),
        "rwkv_a0": nrm(ks[13], (L, RWKV_WIDTH), 0.5),
        "rwkv_a2": nrm(ks[14], (L, RWKV_AAA_RANK, RWKV_WIDTH), RWKV_AAA_RANK ** -0.5),
        "rwkv_g2": nrm(ks[15], (L, RWKV_GATE_RANK, RWKV_WIDTH), RWKV_GATE_RANK ** -0.5),
        "rwkv_k_k": 0.85 + nrm(ks[16], (L, RWKV_WIDTH), 0.05),
        "rwkv_k_a": 1.0 + nrm(ks[17], (L, RWKV_WIDTH), 0.05),
        "rwkv_r_k": nrm(ks[18], (L, RWKV_HEADS, RWKV_N), 0.1),
        "rwkv_ln_w": gain(ks[19], (L, RWKV_WIDTH)),
        "rwkv_ln_b": nrm(ks[20], (L, RWKV_WIDTH), 0.02),
        "w_out": nrm(ks[21], (L, MIX_WIDTH, D_MODEL), MIX_WIDTH ** -0.5),
        "ffn2_norm": gain(ks[22], (L, D_MODEL)),
        "ffn2_w_gate": nrm(ks[23], (L, D_MODEL, D_FF), D_MODEL ** -0.5),
        "ffn2_w_up": nrm(ks[24], (L, D_MODEL, D_FF), D_MODEL ** -0.5),
        "ffn2_w_down": nrm(ks[25], (L, D_FF, D_MODEL), D_FF ** -0.5),
        "final_norm": gain(ks[26], (D_MODEL,)),
    }


def reference(x, ffn1_norm, ffn1_w_gate, ffn1_w_up, ffn1_w_down, mix_norm, w_in,
              gla_alpha_w2, gla_alpha_b, gla_norm, rwkv_mu, rwkv_w0, rwkv_w2, rwkv_a0,
              rwkv_a2, rwkv_g2, rwkv_k_k, rwkv_k_a, rwkv_r_k, rwkv_ln_w, rwkv_ln_b, w_out,
              ffn2_norm, ffn2_w_gate, ffn2_w_up, ffn2_w_down, final_norm):
    for l in range(DEPTH):
        x = x + MACARON_WEIGHT * _swiglu(_rms_norm(x, ffn1_norm[l]), ffn1_w_gate[l], ffn1_w_up[l], ffn1_w_down[l])
        h = _rms_norm(x, mix_norm[l])
        proj = h @ w_in[l]
        gla_p, rwkv_p = jnp.split(proj, [GLA_PROJ], axis=-1)
        q, k, v, g_out, a_lr = _split_cols(gla_p, GLA_COLS)
        o_gla = _gla_mixer(q, k, v, g_out, a_lr, gla_alpha_w2[l], gla_alpha_b[l], gla_norm[l])
        o_rwkv = _rwkv7_mixer(rwkv_p, rwkv_mu[l], rwkv_w0[l], rwkv_w2[l], rwkv_a0[l], rwkv_a2[l],
                              rwkv_g2[l], rwkv_k_k[l], rwkv_k_a[l], rwkv_r_k[l], rwkv_ln_w[l], rwkv_ln_b[l])
        mixed = jnp.concatenate([o_gla, o_rwkv], axis=-1).astype(x.dtype)
        x = x + mixed @ w_out[l]
        x = x + MACARON_WEIGHT * _swiglu(_rms_norm(x, ffn2_norm[l]), ffn2_w_gate[l], ffn2_w_up[l], ffn2_w_down[l])
    return _rms_norm(x, final_norm)
```

```python
import functools

import numpy as np
import jax
import jax.numpy as jnp
from jax import lax
from jax.experimental import pallas as pl
from jax.experimental.pallas import tpu as pltpu

F32 = jnp.float32
BF16 = jnp.bfloat16

NORM_EPS = 1e-6
MACARON_WEIGHT = 0.5
CHUNK = 64
GROUP = 4
LANES = 128
GLA_HEADS = 4
GLA_DK = 64
GLA_DV = 128
GLA_RANK = 16
GLA_GATE_NORMALIZER = 16.0
RWKV_HEADS = 8
RWKV_N = 64
RWKV_GN_EPS = 64e-5
GLA_QK = GLA_HEADS * GLA_DK
GLA_V = GLA_HEADS * GLA_DV
RWKV_W = RWKV_HEADS * RWKV_N
OFF_Q = 0
OFF_K = OFF_Q + GLA_QK
OFF_V = OFF_K + GLA_QK
OFF_GO = OFF_V + GLA_V
OFF_ALR = OFF_GO + GLA_V
OFF_RW = OFF_ALR + LANES
RW_COLS = 3 * RWKV_W + 64 + 64 + 128
RW_R, RW_K, RW_V, RW_LR, RW_G = 0, RWKV_W, 2 * RWKV_W, 3 * RWKV_W, 3 * RWKV_W + LANES
PROJ_PAD = OFF_RW + RW_COLS
N_LEVELS = 6
VMEM_LIMIT_BYTES = 56 * 1024 * 1024


def _dot(a, b):
    return jnp.dot(a.astype(BF16), b.astype(BF16), preferred_element_type=F32)


def _dot_nt(a, b):
    return lax.dot_general(a.astype(BF16), b.astype(BF16), (((1,), (1,)), ((), ())),
                           preferred_element_type=F32)


def _split_dot(m, x):
    hi = x.astype(BF16)
    lo = (x - hi.astype(F32)).astype(BF16)
    return jnp.dot(m, jnp.concatenate([hi, lo], axis=0), preferred_element_type=F32)


def _sigmoid(x):
    return 1.0 / (1.0 + jnp.exp(-x))


def _log1pexp_neg_abs(x):
    return jnp.log(1.0 + jnp.exp(-jnp.abs(x)))


def _rms(x, gain):
    return x * lax.rsqrt(jnp.mean(x * x, axis=-1, keepdims=True) + NORM_EPS) * gain


def _bd_rows(y, width):
    n_tiles = y.shape[1] // LANES
    lane = lax.broadcasted_iota(jnp.int32, (y.shape[0], LANES), 1)
    rows = []
    for h in range(GROUP):
        lo, hi = h * width, (h + 1) * width
        tiles = []
        for j in range(n_tiles):
            tl, th = j * LANES, (j + 1) * LANES
            tile = y[:, tl:th]
            if hi <= tl or lo >= th:
                tiles.append(jnp.zeros_like(tile))
            elif lo <= tl and hi >= th:
                tiles.append(tile)
            elif lo <= tl:
                tiles.append(jnp.where(lane < hi - tl, tile, 0.0))
            else:
                tiles.append(jnp.where(lane >= lo - tl, tile, 0.0))
        rows.append(jnp.concatenate(tiles, axis=1) if n_tiles > 1 else tiles[0])
    return jnp.concatenate(rows, axis=0)


def _bd_keep(x, rows_per_head, width):
    n_tiles = x.shape[1] // LANES
    lane = lax.broadcasted_iota(jnp.int32, (rows_per_head, LANES), 1)
    rows = []
    for h in range(GROUP):
        lo, hi = h * width, (h + 1) * width
        tiles = []
        for j in range(n_tiles):
            tl, th = j * LANES, (j + 1) * LANES
            tile = x[h * rows_per_head:(h + 1) * rows_per_head, tl:th]
            if hi <= tl or lo >= th:
                tiles.append(jnp.zeros_like(tile))
            elif lo <= tl and hi >= th:
                tiles.append(tile)
            elif lo <= tl:
                tiles.append(jnp.where(lane < hi - tl, tile, 0.0))
            else:
                tiles.append(jnp.where(lane >= lo - tl, tile, 0.0))
        rows.append(jnp.concatenate(tiles, axis=1) if n_tiles > 1 else tiles[0])
    return jnp.concatenate(rows, axis=0)


def _pair_sums(x):
    lane = lax.broadcasted_iota(jnp.int32, (x.shape[0], LANES), 1)
    first = lane < RWKV_N
    outs = []
    for p in range(x.shape[1] // LANES):
        xp = x[:, p * LANES:(p + 1) * LANES]
        s0 = jnp.sum(jnp.where(first, xp, 0.0), axis=-1, keepdims=True)
        s1 = jnp.sum(jnp.where(first, 0.0, xp), axis=-1, keepdims=True)
        outs.append(jnp.where(first, s0, s1))
    return jnp.concatenate(outs, axis=1)


def _ffn_kernel(x_ref, gain_ref, wg_ref, wu_ref, wd_ref, fgain_ref, o_ref, *, final_norm):
    x = x_ref[...]
    h = _rms(x, gain_ref[...]).astype(BF16)
    gate = jnp.dot(h, wg_ref[...], preferred_element_type=F32)
    up = jnp.dot(h, wu_ref[...], preferred_element_type=F32)
    act = (gate * _sigmoid(gate) * up).astype(BF16)
    y = x + MACARON_WEIGHT * jnp.dot(act, wd_ref[...], preferred_element_type=F32)
    if final_norm:
        y = _rms(y, fgain_ref[...])
    o_ref[...] = y


def _const_spec(shape):
    return pl.BlockSpec(shape, lambda *_: (0,) * len(shape), pipeline_mode=pl.Buffered(1))


def _ffn(x2d, gain, wg, wu, wd, fgain, *, final_norm, row_tile):
    rows, d = x2d.shape
    dff = wg.shape[1]
    return pl.pallas_call(
        functools.partial(_ffn_kernel, final_norm=final_norm),
        out_shape=jax.ShapeDtypeStruct((rows, d), F32),
        grid=(rows // row_tile,),
        in_specs=[
            pl.BlockSpec((row_tile, d), lambda i: (i, 0)),
            _const_spec((1, d)),
            _const_spec((d, dff)),
            _const_spec((d, dff)),
            _const_spec((dff, d)),
            _const_spec((1, d)),
        ],
        out_specs=pl.BlockSpec((row_tile, d), lambda i: (i, 0)),
        compiler_params=pltpu.CompilerParams(
            dimension_semantics=("arbitrary",), vmem_limit_bytes=VMEM_LIMIT_BYTES),
        name="ffn_final" if final_norm else "ffn",
    )(x2d, gain, wg, wu, wd, fgain)


def _chunk_constants():
    idx = np.arange(CHUNK)
    def tile(m):
        return np.tile(m.astype(np.float32), (1, GROUP))
    masks = [tile(idx[None, :] < idx[:, None])]
    for l in range(1, N_LEVELS + 1):
        same = (idx[:, None] >> l) == (idx[None, :] >> l)
        cross = ((idx[:, None] >> (l - 1)) & 1 == 1) & ((idx[None, :] >> (l - 1)) & 1 == 0)
        masks.append(tile(same & cross))
    masks.append(tile(idx[None, :] <= idx[:, None]))
    masks.append(tile(idx[None, :] == idx[:, None]))
    masks = np.stack(masks)
    ltri = (idx[None, :] <= idx[:, None]).astype(np.float32)
    mats = [ltri]
    for l in range(1, N_LEVELS + 1):
        hblk = idx >> (l - 1)
        half = hblk & 1
        same_h = hblk[:, None] == hblk[None, :]
        eq = same_h & (idx[None, :] <= idx[:, None]) & (half[:, None] == 1)
        ek = same_h & (idx[None, :] > idx[:, None]) & (half[:, None] == 0)
        mats += [eq.astype(np.float32), ek.astype(np.float32)]
    emat = np.concatenate(mats, axis=0)
    return masks, np.concatenate([emat, emat], axis=1), np.concatenate([ltri, ltri], axis=1)


def _mixer_kernel(x_ref, mixg_ref, win_ref, aw2_ref, ab_ref, gnorm_ref, mu_ref, w0_ref, w2_ref,
                  a0_ref, a2_ref, g2_ref, kk_ref, ka_ref, rk_ref, lnw_ref, lnb_ref, wout_ref,
                  mask_ref, emat_ref, ltri_ref,
                  o_ref,
                  proj_ref, carry_ref, r_ref, k_ref, v_ref, lw_ref, av_ref, bv_ref, la_ref,
                  mix_ref, gst_ref, rst_ref, *, tile):
    t_idx = pl.program_id(1)

    @pl.when(t_idx == 0)
    def _():
        carry_ref[...] = jnp.zeros_like(carry_ref)
        gst_ref[...] = jnp.zeros_like(gst_ref)
        rst_ref[...] = jnp.zeros_like(rst_ref)

    x = x_ref[...]
    h = _rms(x, mixg_ref[...]).astype(BF16)
    proj_ref[...] = jnp.dot(h, win_ref[...], preferred_element_type=F32)

    z = _dot(proj_ref[:, OFF_ALR:OFF_ALR + LANES], aw2_ref[...]) + ab_ref[...]
    la_ref[...] = (jnp.minimum(z, 0.0) - _log1pexp_neg_abs(z)) * (1.0 / GLA_GATE_NORMALIZER)

    p = proj_ref[:, OFF_RW:OFF_RW + RW_COLS]
    row = lax.broadcasted_iota(jnp.int32, p.shape, 0)
    prev = jnp.where(row == 0, carry_ref[...], pltpu.roll(p, 1, axis=0))
    carry_ref[...] = p[tile - 1:tile, :]
    p = p + mu_ref[...] * (prev - p)
    r = p[:, RW_R:RW_R + RWKV_W]
    k = p[:, RW_K:RW_K + RWKV_W]
    v = p[:, RW_V:RW_V + RWKV_W]
    lr = p[:, RW_LR:RW_LR + LANES]
    u = w0_ref[...] + _dot(jnp.tanh(lr), w2_ref[...])
    w = jnp.minimum(u, 0.0) - _log1pexp_neg_abs(u) - 0.5
    a = _sigmoid(a0_ref[...] + _dot(lr, a2_ref[...]))
    kk = k * kk_ref[...]
    kk = kk / jnp.maximum(jnp.sqrt(_pair_sums(kk * kk)), 1e-12)
    k = k * (1.0 + (a - 1.0) * ka_ref[...])
    r_ref[...] = r
    k_ref[...] = k
    v_ref[...] = v
    lw_ref[...] = -jnp.exp(w)
    av_ref[...] = -kk
    bv_ref[...] = kk * a
    bonus = _pair_sums(r * k * rk_ref[...]) * v
    gate = _dot(_sigmoid(p[:, RW_G:RW_G + LANES]), g2_ref[...])

    def mask(i):
        return mask_ref[i] > 0.0

    def chunk_body(c, carry):
        rows = pl.ds(pl.multiple_of(c * CHUNK, CHUNK), CHUNK)

        q = proj_ref[rows, OFF_Q:OFF_Q + GLA_QK] * (GLA_DK ** -0.5)
        kg = proj_ref[rows, OFF_K:OFF_K + GLA_QK]
        vg = proj_ref[rows, OFF_V:OFF_V + GLA_V]
        eg = _split_dot(emat_ref[...], la_ref[rows, :])
        cum = eg[0:CHUNK]
        last = cum[CHUNK - 1:CHUNK, :]
        scores = jnp.where(mask(8), _dot_nt(q, _bd_rows(kg, GLA_DK)), 0.0)
        for l in range(1, N_LEVELS + 1):
            base = CHUNK * (2 * l - 1)
            qh = q * jnp.exp(eg[base:base + CHUNK])
            kh = kg * jnp.exp(eg[base + CHUNK:base + 2 * CHUNK])
            scores = scores + jnp.where(mask(l), _dot_nt(qh, _bd_rows(kh, GLA_DK)), 0.0)
        gst = gst_ref[...]
        o = _dot(scores, _bd_rows(vg, GLA_DV)) + _dot(q * jnp.exp(cum), gst)
        kd = kg * jnp.exp(last - cum)
        stack = jnp.concatenate([kd, jnp.broadcast_to(jnp.exp(last), kd.shape)], axis=0)
        stack_t = stack.T
        decay_col = stack_t[:, CHUNK:CHUNK + 1]
        vpad = jnp.concatenate([vg, jnp.zeros_like(vg)], axis=0)
        gst_ref[...] = gst * decay_col + _bd_keep(_dot(stack_t, vpad), GLA_DK, GLA_DV)
        for hd in range(GLA_HEADS):
            sl = slice(hd * GLA_DV, (hd + 1) * GLA_DV)
            oh = o[:, sl]
            oh = oh * lax.rsqrt(jnp.mean(oh * oh, axis=-1, keepdims=True) + NORM_EPS) * gnorm_ref[...]
            go = proj_ref[rows, OFF_GO + hd * GLA_DV:OFF_GO + (hd + 1) * GLA_DV]
            mix_ref[rows, sl] = oh * (go * _sigmoid(go))

        cum_all = _split_dot(ltri_ref[...], lw_ref[rows, :])
        for g in range(RWKV_HEADS // GROUP):
            gl = slice(g * GROUP * RWKV_N, (g + 1) * GROUP * RWKV_N)
            rc, kc, vc = r_ref[rows, gl], k_ref[rows, gl], v_ref[rows, gl]
            lw, ac, bc = lw_ref[rows, gl], av_ref[rows, gl], bv_ref[rows, gl]
            cum = cum_all[:, gl]
            last = cum[CHUNK - 1:CHUNK, :]
            e_neg = jnp.exp(-cum)
            e_last = jnp.exp(last - cum)
            at = ac * jnp.exp(cum - lw)
            rt = rc * jnp.exp(cum)
            ar = jnp.concatenate([at, rt], axis=0)
            ab_ = _dot_nt(ar, _bd_rows(bc * e_neg, RWKV_N))
            ak_ = _dot_nt(ar, _bd_rows(kc * e_neg, RWKV_N))
            a_ab = jnp.where(mask(0), ab_[0:CHUNK], 0.0)
            a_ak = jnp.where(mask(0), ak_[0:CHUNK], 0.0)
            a_rb = jnp.where(mask(7), ab_[CHUNK:], 0.0)
            a_rk = jnp.where(mask(7), ak_[CHUNK:], 0.0)
            tinv = jnp.where(mask(8), 1.0, 0.0) + jnp.where(mask(1), a_ab, 0.0)
            for l in range(2, N_LEVELS + 1):
                ll = jnp.where(mask(l), a_ab, 0.0)
                tinv = tinv + _dot(tinv, _bd_rows(_dot(ll, _bd_rows(tinv, CHUNK)), CHUNK))
            vbd = _bd_rows(vc, RWKV_N)
            akv = _dot(a_ak, vbd)
            tw = _dot(tinv, jnp.concatenate([_bd_rows(at, RWKV_N), _bd_rows(akv, RWKV_N)], axis=1))
            ap, uv = tw[:, 0:GROUP * RWKV_N], tw[:, GROUP * RWKV_N:]
            rp = rt + _dot(a_rb, _bd_rows(ap, RWKV_N))
            yv = _dot(jnp.concatenate([a_rb, a_rk], axis=1),
                      jnp.concatenate([_bd_rows(uv, RWKV_N), vbd], axis=0))
            bk_t = jnp.concatenate([bc * e_last, kc * e_last], axis=0).T
            rhs = jnp.concatenate([jnp.concatenate([ap, uv], axis=1),
                                   jnp.concatenate([jnp.zeros_like(vc), vc], axis=1)], axis=0)
            mn = _dot(bk_t, rhs)
            rowi = lax.broadcasted_iota(jnp.int32, (GROUP * RWKV_N, GROUP * RWKV_N), 0)
            coli = lax.broadcasted_iota(jnp.int32, (GROUP * RWKV_N, GROUP * RWKV_N), 1)
            diag = jnp.where(rowi == coli, jnp.exp(last), 0.0)
            mc = _bd_keep(mn[:, 0:GROUP * RWKV_N], RWKV_N, RWKV_N) + diag
            nc = _bd_keep(mn[:, GROUP * RWKV_N:], RWKV_N, RWKV_N)
            st = rst_ref[g]
            mix_ref[rows, GLA_V + g * GROUP * RWKV_N:GLA_V + (g + 1) * GROUP * RWKV_N] = _dot(rp, st) + yv
            rst_ref[g] = _dot(mc, st) + nc
        return carry

    lax.fori_loop(0, tile // CHUNK, chunk_body, 0)

    y = mix_ref[:, GLA_V:]
    mean = _pair_sums(y) * (1.0 / RWKV_N)
    yc = y - mean
    var = _pair_sums(yc * yc) * (1.0 / RWKV_N)
    y = yc * lax.rsqrt(var + RWKV_GN_EPS) * lnw_ref[...] + lnb_ref[...]
    mix_ref[:, GLA_V:] = (y + bonus) * gate
    o_ref[...] = x + jnp.dot(mix_ref[...].astype(BF16), wout_ref[...], preferred_element_type=F32)


def _mixer(x, mixg, win, aw2, ab, gnorm, mu, w0, w2, a0, a2, g2, kk, ka, rk, lnw, lnb, wout, *, tile):
    bsz, seq, d = x.shape
    masks, emat, ltri = _chunk_constants()
    consts = [jnp.asarray(masks), jnp.asarray(emat, BF16), jnp.asarray(ltri, BF16)]
    small = [mixg, win, aw2, ab, gnorm, mu, w0, w2, a0, a2, g2, kk, ka, rk, lnw, lnb, wout] + consts
    xspec = pl.BlockSpec((None, tile, d), lambda b, t: (b, t, 0))
    return pl.pallas_call(
        functools.partial(_mixer_kernel, tile=tile),
        out_shape=jax.ShapeDtypeStruct((bsz, seq, d), F32),
        grid=(bsz, seq // tile),
        in_specs=[xspec] + [_const_spec(a.shape) for a in small],
        out_specs=xspec,
        scratch_shapes=[
            pltpu.VMEM((tile, PROJ_PAD), F32),
            pltpu.VMEM((1, RW_COLS), F32),
            pltpu.VMEM((tile, RWKV_W), F32),
            pltpu.VMEM((tile, RWKV_W), F32),
            pltpu.VMEM((tile, RWKV_W), F32),
            pltpu.VMEM((tile, RWKV_W), F32),
            pltpu.VMEM((tile, RWKV_W), F32),
            pltpu.VMEM((tile, RWKV_W), F32),
            pltpu.VMEM((tile, GLA_QK), F32),
            pltpu.VMEM((tile, GLA_V + RWKV_W), F32),
            pltpu.VMEM((GROUP * GLA_DK, GLA_V), F32),
            pltpu.VMEM((RWKV_HEADS // GROUP, GROUP * RWKV_N, GROUP * RWKV_N), F32),
        ],
        compiler_params=pltpu.CompilerParams(
            dimension_semantics=("arbitrary", "arbitrary"), vmem_limit_bytes=VMEM_LIMIT_BYTES),
        name="mixer",
    )(x, *small)


def _row(vec):
    return vec.reshape(1, -1).astype(F32)


def kernel(x, ffn1_norm, ffn1_w_gate, ffn1_w_up, ffn1_w_down, mix_norm, w_in, gla_alpha_w2, gla_alpha_b, gla_norm, rwkv_mu, rwkv_w0, rwkv_w2, rwkv_a0, rwkv_a2, rwkv_g2, rwkv_k_k, rwkv_k_a, rwkv_r_k, rwkv_ln_w, rwkv_ln_b, w_out, ffn2_norm, ffn2_w_gate, ffn2_w_up, ffn2_w_down, final_norm):
    bsz, seq, d = x.shape
    depth = ffn1_norm.shape[0]
    row_tile = min(512, bsz * seq)
    tile = min(256, seq)
    ones = jnp.ones((1, d), F32)
    gla_cols = OFF_ALR + GLA_RANK
    for l in range(depth):
        x2 = _ffn(x.reshape(bsz * seq, d), _row(ffn1_norm[l]), ffn1_w_gate[l].astype(BF16),
                  ffn1_w_up[l].astype(BF16), ffn1_w_down[l].astype(BF16), ones,
                  final_norm=False, row_tile=row_tile).reshape(bsz, seq, d)
        win = jnp.concatenate([w_in[l][:, :gla_cols], jnp.zeros((d, LANES - GLA_RANK), F32),
                               w_in[l][:, gla_cols:]], axis=1).astype(BF16)
        aw2 = jnp.concatenate([gla_alpha_w2[l], jnp.zeros((LANES - GLA_RANK, GLA_QK), F32)], axis=0).astype(BF16)
        zeros64 = jnp.zeros((64, RWKV_W), F32)
        w2 = jnp.concatenate([rwkv_w2[l], zeros64], axis=0).astype(BF16)
        a2 = jnp.concatenate([zeros64, rwkv_a2[l]], axis=0).astype(BF16)
        x3 = _mixer(x2, _row(mix_norm[l]), win, aw2, _row(gla_alpha_b[l]), _row(gla_norm[l]),
                    _row(rwkv_mu[l]), _row(rwkv_w0[l]), w2, _row(rwkv_a0[l]), a2, rwkv_g2[l].astype(BF16),
                    _row(rwkv_k_k[l]), _row(rwkv_k_a[l]), _row(rwkv_r_k[l]), _row(rwkv_ln_w[l]),
                    _row(rwkv_ln_b[l]), w_out[l].astype(BF16), tile=tile)
        last = l == depth - 1
        x = _ffn(x3.reshape(bsz * seq, d), _row(ffn2_norm[l]), ffn2_w_gate[l].astype(BF16),
                 ffn2_w_up[l].astype(BF16), ffn2_w_down[l].astype(BF16), _row(final_norm) if last else ones,
                 final_norm=last, row_tile=row_tile).reshape(bsz, seq, d)
    return x
```

```python
import functools

import numpy as np
import jax
import jax.numpy as jnp
from jax import lax
from jax.experimental import pallas as pl
from jax.experimental.pallas import tpu as pltpu

F32 = jnp.float32
BF16 = jnp.bfloat16

NORM_EPS = 1e-6
MACARON_WEIGHT = 0.5
CHUNK = 64
GROUP = 4
LANES = 128
GLA_HEADS = 4
GLA_DK = 64
GLA_DV = 128
GLA_RANK = 16
GLA_GATE_NORMALIZER = 16.0
RWKV_HEADS = 8
RWKV_N = 64
RWKV_GN_EPS = 64e-5
GLA_QK = GLA_HEADS * GLA_DK
GLA_V = GLA_HEADS * GLA_DV
RWKV_W = RWKV_HEADS * RWKV_N
OFF_Q = 0
OFF_K = OFF_Q + GLA_QK
OFF_V = OFF_K + GLA_QK
OFF_GO = OFF_V + GLA_V
OFF_ALR = OFF_GO + GLA_V
OFF_RW = OFF_ALR + LANES
RW_COLS = 3 * RWKV_W + 64 + 64 + 128
RW_R, RW_K, RW_V, RW_LR, RW_G = 0, RWKV_W, 2 * RWKV_W, 3 * RWKV_W, 3 * RWKV_W + LANES
PROJ_PAD = OFF_RW + RW_COLS
N_LEVELS = 6
VMEM_LIMIT_BYTES = 56 * 1024 * 1024


def _dot(a, b):
    return jnp.dot(a.astype(BF16), b.astype(BF16), preferred_element_type=F32)


def _dot_nt(a, b):
    return lax.dot_general(a.astype(BF16), b.astype(BF16), (((1,), (1,)), ((), ())),
                           preferred_element_type=F32)


def _split_dot(m, x):
    hi = x.astype(BF16)
    lo = (x - hi.astype(F32)).astype(BF16)
    return jnp.dot(m, jnp.concatenate([hi, lo], axis=0), preferred_element_type=F32)


def _sigmoid(x):
    return 1.0 / (1.0 + jnp.exp(-x))


def _log1pexp_neg_abs(x):
    return jnp.log(1.0 + jnp.exp(-jnp.abs(x)))


def _rms(x, gain):
    return x * lax.rsqrt(jnp.mean(x * x, axis=-1, keepdims=True) + NORM_EPS) * gain


def _bd_rows(y, width):
    n_tiles = y.shape[1] // LANES
    lane = lax.broadcasted_iota(jnp.int32, (y.shape[0], LANES), 1)
    rows = []
    for h in range(GROUP):
        lo, hi = h * width, (h + 1) * width
        tiles = []
        for j in range(n_tiles):
            tl, th = j * LANES, (j + 1) * LANES
            tile = y[:, tl:th]
            if hi <= tl or lo >= th:
                tiles.append(jnp.zeros_like(tile))
            elif lo <= tl and hi >= th:
                tiles.append(tile)
            elif lo <= tl:
                tiles.append(jnp.where(lane < hi - tl, tile, 0.0))
            else:
                tiles.append(jnp.where(lane >= lo - tl, tile, 0.0))
        rows.append(jnp.concatenate(tiles, axis=1) if n_tiles > 1 else tiles[0])
    return jnp.concatenate(rows, axis=0)


def _bd_keep(x, rows_per_head, width):
    n_tiles = x.shape[1] // LANES
    lane = lax.broadcasted_iota(jnp.int32, (rows_per_head, LANES), 1)
    rows = []
    for h in range(GROUP):
        lo, hi = h * width, (h + 1) * width
        tiles = []
        for j in range(n_tiles):
            tl, th = j * LANES, (j + 1) * LANES
            tile = x[h * rows_per_head:(h + 1) * rows_per_head, tl:th]
            if hi <= tl or lo >= th:
                tiles.append(jnp.zeros_like(tile))
            elif lo <= tl and hi >= th:
                tiles.append(tile)
            elif lo <= tl:
                tiles.append(jnp.where(lane < hi - tl, tile, 0.0))
            else:
                tiles.append(jnp.where(lane >= lo - tl, tile, 0.0))
        rows.append(jnp.concatenate(tiles, axis=1) if n_tiles > 1 else tiles[0])
    return jnp.concatenate(rows, axis=0)


def _pair_sums(x):
    lane = lax.broadcasted_iota(jnp.int32, (x.shape[0], LANES), 1)
    first = lane < RWKV_N
    outs = []
    for p in range(x.shape[1] // LANES):
        xp = x[:, p * LANES:(p + 1) * LANES]
        s0 = jnp.sum(jnp.where(first, xp, 0.0), axis=-1, keepdims=True)
        s1 = jnp.sum(jnp.where(first, 0.0, xp), axis=-1, keepdims=True)
        outs.append(jnp.where(first, s0, s1))
    return jnp.concatenate(outs, axis=1)


def _ffn_kernel(x_ref, gain_ref, wg_ref, wu_ref, wd_ref, fgain_ref, o_ref, *, final_norm):
    x = x_ref[...]
    h = _rms(x, gain_ref[...]).astype(BF16)
    gate = jnp.dot(h, wg_ref[...], preferred_element_type=F32)
    up = jnp.dot(h, wu_ref[...], preferred_element_type=F32)
    act = (gate * _sigmoid(gate) * up).astype(BF16)
    y = x + MACARON_WEIGHT * jnp.dot(act, wd_ref[...], preferred_element_type=F32)
    if final_norm:
        y = _rms(y, fgain_ref[...])
    o_ref[...] = y


def _const_spec(shape):
    return pl.BlockSpec(shape, lambda *_: (0,) * len(shape), pipeline_mode=pl.Buffered(1))


def _ffn(x2d, gain, wg, wu, wd, fgain, *, final_norm, row_tile):
    rows, d = x2d.shape
    dff = wg.shape[1]
    return pl.pallas_call(
        functools.partial(_ffn_kernel, final_norm=final_norm),
        out_shape=jax.ShapeDtypeStruct((rows, d), F32),
        grid=(rows // row_tile,),
        in_specs=[
            pl.BlockSpec((row_tile, d), lambda i: (i, 0)),
            _const_spec((1, d)),
            _const_spec((d, dff)),
            _const_spec((d, dff)),
            _const_spec((dff, d)),
            _const_spec((1, d)),
        ],
        out_specs=pl.BlockSpec((row_tile, d), lambda i: (i, 0)),
        compiler_params=pltpu.CompilerParams(
            dimension_semantics=("arbitrary",), vmem_limit_bytes=VMEM_LIMIT_BYTES),
        name="ffn_final" if final_norm else "ffn",
    )(x2d, gain, wg, wu, wd, fgain)


def _chunk_constants():
    idx = np.arange(CHUNK)
    def tile(m):
        return np.tile(m.astype(np.float32), (1, GROUP))
    masks = [tile(idx[None, :] < idx[:, None])]
    for l in range(1, N_LEVELS + 1):
        same = (idx[:, None] >> l) == (idx[None, :] >> l)
        cross = ((idx[:, None] >> (l - 1)) & 1 == 1) & ((idx[None, :] >> (l - 1)) & 1 == 0)
        masks.append(tile(same & cross))
    masks.append(tile(idx[None, :] <= idx[:, None]))
    masks.append(tile(idx[None, :] == idx[:, None]))
    masks = np.stack(masks)
    ltri = (idx[None, :] <= idx[:, None]).astype(np.float32)
    mats = [ltri]
    for l in range(1, N_LEVELS + 1):
        hblk = idx >> (l - 1)
        half = hblk & 1
        same_h = hblk[:, None] == hblk[None, :]
        eq = same_h & (idx[None, :] <= idx[:, None]) & (half[:, None] == 1)
        ek = same_h & (idx[None, :] > idx[:, None]) & (half[:, None] == 0)
        mats += [eq.astype(np.float32), ek.astype(np.float32)]
    emat = np.concatenate(mats, axis=0)
    return masks, np.concatenate([emat, emat], axis=1), np.concatenate([ltri, ltri], axis=1)


def _mixer_kernel(x_ref, mixg_ref, win_ref, aw2_ref, ab_ref, gnorm_ref, mu_ref, w0_ref, w2_ref,
                  a0_ref, a2_ref, g2_ref, kk_ref, ka_ref, rk_ref, lnw_ref, lnb_ref, wout_ref,
                  mask_ref, emat_ref, ltri_ref,
                  o_ref,
                  proj_ref, carry_ref, r_ref, k_ref, v_ref, lw_ref, av_ref, bv_ref, la_ref,
                  mix_ref, gst_ref, rst_ref, *, tile):
    t_idx = pl.program_id(1)

    @pl.when(t_idx == 0)
    def _():
        carry_ref[...] = jnp.zeros_like(carry_ref)
        gst_ref[...] = jnp.zeros_like(gst_ref)
        rst_ref[...] = jnp.zeros_like(rst_ref)

    x = x_ref[...]
    h = _rms(x, mixg_ref[...]).astype(BF16)
    proj_ref[...] = jnp.dot(h, win_ref[...], preferred_element_type=F32)

    z = _dot(proj_ref[:, OFF_ALR:OFF_ALR + LANES], aw2_ref[...]) + ab_ref[...]
    la_ref[...] = (jnp.minimum(z, 0.0) - _log1pexp_neg_abs(z)) * (1.0 / GLA_GATE_NORMALIZER)

    p = proj_ref[:, OFF_RW:OFF_RW + RW_COLS]
    row = lax.broadcasted_iota(jnp.int32, p.shape, 0)
    prev = jnp.where(row == 0, carry_ref[...], pltpu.roll(p, 1, axis=0))
    carry_ref[...] = p[tile - 1:tile, :]
    p = p + mu_ref[...] * (prev - p)
    r = p[:, RW_R:RW_R + RWKV_W]
    k = p[:, RW_K:RW_K + RWKV_W]
    v = p[:, RW_V:RW_V + RWKV_W]
    lr = p[:, RW_LR:RW_LR + LANES]
    u = w0_ref[...] + _dot(jnp.tanh(lr), w2_ref[...])
    w = jnp.minimum(u, 0.0) - _log1pexp_neg_abs(u) - 0.5
    a = _sigmoid(a0_ref[...] + _dot(lr, a2_ref[...]))
    kk = k * kk_ref[...]
    kk = kk / jnp.maximum(jnp.sqrt(_pair_sums(kk * kk)), 1e-12)
    k = k * (1.0 + (a - 1.0) * ka_ref[...])
    r_ref[...] = r
    k_ref[...] = k
    v_ref[...] = v
    lw_ref[...] = -jnp.exp(w)
    av_ref[...] = -kk
    bv_ref[...] = kk * a
    bonus = _pair_sums(r * k * rk_ref[...]) * v
    gate = _dot(_sigmoid(p[:, RW_G:RW_G + LANES]), g2_ref[...])

    def mask(i):
        return mask_ref[i] > 0.0

    n_chunks = tile // CHUNK
    chunk_rows = [slice(c * CHUNK, (c + 1) * CHUNK) for c in range(n_chunks)]

    gla = []
    for rows in chunk_rows:
        d = {}
        d["q"] = proj_ref[rows, OFF_Q:OFF_Q + GLA_QK] * (GLA_DK ** -0.5)
        d["k"] = proj_ref[rows, OFF_K:OFF_K + GLA_QK]
        d["v"] = proj_ref[rows, OFF_V:OFF_V + GLA_V]
        d["eg"] = _split_dot(emat_ref[...], la_ref[rows, :])
        gla.append(d)
    for d in gla:
        d["scores"] = jnp.where(mask(8), _dot_nt(d["q"], _bd_rows(d["k"], GLA_DK)), 0.0)
    for l in range(1, N_LEVELS + 1):
        base = CHUNK * (2 * l - 1)
        for d in gla:
            qh = d["q"] * jnp.exp(d["eg"][base:base + CHUNK])
            kh = d["k"] * jnp.exp(d["eg"][base + CHUNK:base + 2 * CHUNK])
            d["scores"] = d["scores"] + jnp.where(mask(l), _dot_nt(qh, _bd_rows(kh, GLA_DK)), 0.0)
    for d in gla:
        cum = d["eg"][0:CHUNK]
        last = cum[CHUNK - 1:CHUNK, :]
        d["o"] = _dot(d["scores"], _bd_rows(d["v"], GLA_DV))
        d["qe"] = d["q"] * jnp.exp(cum)
        kd = d["k"] * jnp.exp(last - cum)
        stack_t = jnp.concatenate([kd, jnp.broadcast_to(jnp.exp(last), kd.shape)], axis=0).T
        d["decay_col"] = stack_t[:, CHUNK:CHUNK + 1]
        vpad = jnp.concatenate([d["v"], jnp.zeros_like(d["v"])], axis=0)
        d["kv"] = _bd_keep(_dot(stack_t, vpad), GLA_DK, GLA_DV)
    gst = gst_ref[...]
    for rows, d in zip(chunk_rows, gla):
        o = d["o"] + _dot(d["qe"], gst)
        gst = gst * d["decay_col"] + d["kv"]
        for hd in range(GLA_HEADS):
            sl = slice(hd * GLA_DV, (hd + 1) * GLA_DV)
            oh = o[:, sl]
            oh = oh * lax.rsqrt(jnp.mean(oh * oh, axis=-1, keepdims=True) + NORM_EPS) * gnorm_ref[...]
            go = proj_ref[rows, OFF_GO + hd * GLA_DV:OFF_GO + (hd + 1) * GLA_DV]
            mix_ref[rows, sl] = oh * (go * _sigmoid(go))
    gst_ref[...] = gst

    n_groups = RWKV_HEADS // GROUP
    gw = GROUP * RWKV_N
    items = []
    for rows in chunk_rows:
        cum_all = _split_dot(ltri_ref[...], lw_ref[rows, :])
        for g in range(n_groups):
            gl = slice(g * gw, (g + 1) * gw)
            rc, kc, vc = r_ref[rows, gl], k_ref[rows, gl], v_ref[rows, gl]
            lw, ac, bc = lw_ref[rows, gl], av_ref[rows, gl], bv_ref[rows, gl]
            cum = cum_all[:, gl]
            last = cum[CHUNK - 1:CHUNK, :]
            e_neg = jnp.exp(-cum)
            e_last = jnp.exp(last - cum)
            d = {"rows": rows, "g": g, "v": vc, "e_last_row": jnp.exp(last)}
            d["at"] = ac * jnp.exp(cum - lw)
            d["rt"] = rc * jnp.exp(cum)
            d["bt_bd"] = _bd_rows(bc * e_neg, RWKV_N)
            d["kt_bd"] = _bd_rows(kc * e_neg, RWKV_N)
            d["bk_t"] = jnp.concatenate([bc * e_last, kc * e_last], axis=0).T
            d["vbd"] = _bd_rows(vc, RWKV_N)
            items.append(d)
    for d in items:
        ar = jnp.concatenate([d["at"], d["rt"]], axis=0)
        ab_ = _dot_nt(ar, d["bt_bd"])
        ak_ = _dot_nt(ar, d["kt_bd"])
        d["a_ab"] = jnp.where(mask(0), ab_[0:CHUNK], 0.0)
        d["a_ak"] = jnp.where(mask(0), ak_[0:CHUNK], 0.0)
        d["a_rb"] = jnp.where(mask(7), ab_[CHUNK:], 0.0)
        d["a_rk"] = jnp.where(mask(7), ak_[CHUNK:], 0.0)
        d["tinv"] = jnp.where(mask(8), 1.0, 0.0) + jnp.where(mask(1), d["a_ab"], 0.0)
    for d in items:
        d["akv"] = _dot(d["a_ak"], d["vbd"])
    for l in range(2, N_LEVELS + 1):
        for d in items:
            d["x"] = _dot(jnp.where(mask(l), d["a_ab"], 0.0), _bd_rows(d["tinv"], CHUNK))
        for d in items:
            d["tinv"] = d["tinv"] + _dot(d["tinv"], _bd_rows(d["x"], CHUNK))
    for d in items:
        tw = _dot(d["tinv"], jnp.concatenate([_bd_rows(d["at"], RWKV_N), _bd_rows(d["akv"], RWKV_N)], axis=1))
        d["ap"], d["uv"] = tw[:, 0:gw], tw[:, gw:]
    for d in items:
        d["rp"] = d["rt"] + _dot(d["a_rb"], _bd_rows(d["ap"], RWKV_N))
        d["yv"] = _dot(jnp.concatenate([d["a_rb"], d["a_rk"]], axis=1),
                       jnp.concatenate([_bd_rows(d["uv"], RWKV_N), d["vbd"]], axis=0))
    rowi = lax.broadcasted_iota(jnp.int32, (gw, gw), 0)
    coli = lax.broadcasted_iota(jnp.int32, (gw, gw), 1)
    for d in items:
        rhs = jnp.concatenate([jnp.concatenate([d["ap"], d["uv"]], axis=1),
                               jnp.concatenate([jnp.zeros_like(d["v"]), d["v"]], axis=1)], axis=0)
        mn = _dot(d["bk_t"], rhs)
        d["mc"] = _bd_keep(mn[:, 0:gw], RWKV_N, RWKV_N) + jnp.where(rowi == coli, d["e_last_row"], 0.0)
        d["nc"] = _bd_keep(mn[:, gw:], RWKV_N, RWKV_N)
    states = [rst_ref[g] for g in range(n_groups)]
    for d in items:
        g = d["g"]
        mix_ref[d["rows"], GLA_V + g * gw:GLA_V + (g + 1) * gw] = _dot(d["rp"], states[g]) + d["yv"]
        states[g] = _dot(d["mc"], states[g]) + d["nc"]
    for g in range(n_groups):
        rst_ref[g] = states[g]

    y = mix_ref[:, GLA_V:]
    mean = _pair_sums(y) * (1.0 / RWKV_N)
    yc = y - mean
    var = _pair_sums(yc * yc) * (1.0 / RWKV_N)
    y = yc * lax.rsqrt(var + RWKV_GN_EPS) * lnw_ref[...] + lnb_ref[...]
    mix_ref[:, GLA_V:] = (y + bonus) * gate
    o_ref[...] = x + jnp.dot(mix_ref[...].astype(BF16), wout_ref[...], preferred_element_type=F32)


def _mixer(x, mixg, win, aw2, ab, gnorm, mu, w0, w2, a0, a2, g2, kk, ka, rk, lnw, lnb, wout, *, tile):
    bsz, seq, d = x.shape
    masks, emat, ltri = _chunk_constants()
    consts = [jnp.asarray(masks), jnp.asarray(emat, BF16), jnp.asarray(ltri, BF16)]
    small = [mixg, win, aw2, ab, gnorm, mu, w0, w2, a0, a2, g2, kk, ka, rk, lnw, lnb, wout] + consts
    xspec = pl.BlockSpec((None, tile, d), lambda b, t: (b, t, 0))
    return pl.pallas_call(
        functools.partial(_mixer_kernel, tile=tile),
        out_shape=jax.ShapeDtypeStruct((bsz, seq, d), F32),
        grid=(bsz, seq // tile),
        in_specs=[xspec] + [_const_spec(a.shape) for a in small],
        out_specs=xspec,
        scratch_shapes=[
            pltpu.VMEM((tile, PROJ_PAD), F32),
            pltpu.VMEM((1, RW_COLS), F32),
            pltpu.VMEM((tile, RWKV_W), F32),
            pltpu.VMEM((tile, RWKV_W), F32),
            pltpu.VMEM((tile, RWKV_W), F32),
            pltpu.VMEM((tile, RWKV_W), F32),
            pltpu.VMEM((tile, RWKV_W), F32),
            pltpu.VMEM((tile, RWKV_W), F32),
            pltpu.VMEM((tile, GLA_QK), F32),
            pltpu.VMEM((tile, GLA_V + RWKV_W), F32),
            pltpu.VMEM((GROUP * GLA_DK, GLA_V), F32),
            pltpu.VMEM((RWKV_HEADS // GROUP, GROUP * RWKV_N, GROUP * RWKV_N), F32),
        ],
        compiler_params=pltpu.CompilerParams(
            dimension_semantics=("arbitrary", "arbitrary"), vmem_limit_bytes=VMEM_LIMIT_BYTES),
        name="mixer",
    )(x, *small)


def _row(vec):
    return vec.reshape(1, -1).astype(F32)


def kernel(x, ffn1_norm, ffn1_w_gate, ffn1_w_up, ffn1_w_down, mix_norm, w_in, gla_alpha_w2, gla_alpha_b, gla_norm, rwkv_mu, rwkv_w0, rwkv_w2, rwkv_a0, rwkv_a2, rwkv_g2, rwkv_k_k, rwkv_k_a, rwkv_r_k, rwkv_ln_w, rwkv_ln_b, w_out, ffn2_norm, ffn2_w_gate, ffn2_w_up, ffn2_w_down, final_norm):
    bsz, seq, d = x.shape
    depth = ffn1_norm.shape[0]
    row_tile = min(512, bsz * seq)
    tile = min(256, seq)
    ones = jnp.ones((1, d), F32)
    gla_cols = OFF_ALR + GLA_RANK
    for l in range(depth):
        x2 = _ffn(x.reshape(bsz * seq, d), _row(ffn1_norm[l]), ffn1_w_gate[l].astype(BF16),
                  ffn1_w_up[l].astype(BF16), ffn1_w_down[l].astype(BF16), ones,
                  final_norm=False, row_tile=row_tile).reshape(bsz, seq, d)
        win = jnp.concatenate([w_in[l][:, :gla_cols], jnp.zeros((d, LANES - GLA_RANK), F32),
                               w_in[l][:, gla_cols:]], axis=1).astype(BF16)
        aw2 = jnp.concatenate([gla_alpha_w2[l], jnp.zeros((LANES - GLA_RANK, GLA_QK), F32)], axis=0).astype(BF16)
        zeros64 = jnp.zeros((64, RWKV_W), F32)
        w2 = jnp.concatenate([rwkv_w2[l], zeros64], axis=0).astype(BF16)
        a2 = jnp.concatenate([zeros64, rwkv_a2[l]], axis=0).astype(BF16)
        x3 = _mixer(x2, _row(mix_norm[l]), win, aw2, _row(gla_alpha_b[l]), _row(gla_norm[l]),
                    _row(rwkv_mu[l]), _row(rwkv_w0[l]), w2, _row(rwkv_a0[l]), a2, rwkv_g2[l].astype(BF16),
                    _row(rwkv_k_k[l]), _row(rwkv_k_a[l]), _row(rwkv_r_k[l]), _row(rwkv_ln_w[l]),
                    _row(rwkv_ln_b[l]), w_out[l].astype(BF16), tile=tile)
        last = l == depth - 1
        x = _ffn(x3.reshape(bsz * seq, d), _row(ffn2_norm[l]), ffn2_w_gate[l].astype(BF16),
                 ffn2_w_up[l].astype(BF16), ffn2_w_down[l].astype(BF16), _row(final_norm) if last else ones,
                 final_norm=last, row_tile=row_tile).reshape(bsz, seq, d)
    return x
```

```python
import functools

import numpy as np
import jax
import jax.numpy as jnp
from jax import lax
from jax.experimental import pallas as pl
from jax.experimental.pallas import tpu as pltpu

F32 = jnp.float32
BF16 = jnp.bfloat16

NORM_EPS = 1e-6
MACARON_WEIGHT = 0.5
CHUNK = 64
GROUP = 4
LANES = 128
GLA_HEADS = 4
GLA_DK = 64
GLA_DV = 128
GLA_RANK = 16
GLA_GATE_NORMALIZER = 16.0
RWKV_HEADS = 8
RWKV_N = 64
RWKV_GN_EPS = 64e-5
GLA_QK = GLA_HEADS * GLA_DK
GLA_V = GLA_HEADS * GLA_DV
RWKV_W = RWKV_HEADS * RWKV_N
OFF_Q = 0
OFF_K = OFF_Q + GLA_QK
OFF_V = OFF_K + GLA_QK
OFF_GO = OFF_V + GLA_V
OFF_ALR = OFF_GO + GLA_V
OFF_RW = OFF_ALR + LANES
RW_COLS = 3 * RWKV_W + 64 + 64 + 128
RW_R, RW_K, RW_V, RW_LR, RW_G = 0, RWKV_W, 2 * RWKV_W, 3 * RWKV_W, 3 * RWKV_W + LANES
PROJ_PAD = OFF_RW + RW_COLS
N_LEVELS = 6
FFN_COLS = 256
VMEM_LIMIT_BYTES = 56 * 1024 * 1024


def _dot(a, b):
    return jnp.dot(a.astype(BF16), b.astype(BF16), preferred_element_type=F32)


def _dot_nt(a, b):
    return lax.dot_general(a.astype(BF16), b.astype(BF16), (((1,), (1,)), ((), ())),
                           preferred_element_type=F32)


def _split_dot(m, x):
    hi = x.astype(BF16)
    lo = (x - hi.astype(F32)).astype(BF16)
    return jnp.dot(m, jnp.concatenate([hi, lo], axis=0), preferred_element_type=F32)


def _sigmoid(x):
    return 1.0 / (1.0 + jnp.exp(-x))


def _log1pexp_neg_abs(x):
    return jnp.log(1.0 + jnp.exp(-jnp.abs(x)))


def _rms(x, gain):
    return x * lax.rsqrt(jnp.mean(x * x, axis=-1, keepdims=True) + NORM_EPS) * gain


def _bd_rows(y, width):
    n_tiles = y.shape[1] // LANES
    lane = lax.broadcasted_iota(jnp.int32, (y.shape[0], LANES), 1)
    rows = []
    for h in range(GROUP):
        lo, hi = h * width, (h + 1) * width
        tiles = []
        for j in range(n_tiles):
            tl, th = j * LANES, (j + 1) * LANES
            tile = y[:, tl:th]
            if hi <= tl or lo >= th:
                tiles.append(jnp.zeros_like(tile))
            elif lo <= tl and hi >= th:
                tiles.append(tile)
            elif lo <= tl:
                tiles.append(jnp.where(lane < hi - tl, tile, 0.0))
            else:
                tiles.append(jnp.where(lane >= lo - tl, tile, 0.0))
        rows.append(jnp.concatenate(tiles, axis=1) if n_tiles > 1 else tiles[0])
    return jnp.concatenate(rows, axis=0)


def _stack_to_bd(s, rows_per_head):
    rows = []
    for h in range(GROUP):
        blk = s[h * rows_per_head:(h + 1) * rows_per_head, :]
        rows.append(jnp.concatenate([blk if j == h else jnp.zeros_like(blk) for j in range(GROUP)], axis=1))
    return jnp.concatenate(rows, axis=0)


def _pair_blocks_to_bd(tiles):
    half = tiles[0].shape[0] // 2
    lane = lax.broadcasted_iota(jnp.int32, (half, LANES), 1)
    rows = []
    for p, t in enumerate(tiles):
        kept = jnp.concatenate([jnp.where(lane < half, t[0:half], 0.0),
                                jnp.where(lane >= half, t[half:], 0.0)], axis=0)
        rows.append(jnp.concatenate([kept if j == p else jnp.zeros_like(kept) for j in range(len(tiles))], axis=1))
    return jnp.concatenate(rows, axis=0)


def _pair_sums(x):
    lane = lax.broadcasted_iota(jnp.int32, (x.shape[0], LANES), 1)
    first = lane < RWKV_N
    outs = []
    for p in range(x.shape[1] // LANES):
        xp = x[:, p * LANES:(p + 1) * LANES]
        s0 = jnp.sum(jnp.where(first, xp, 0.0), axis=-1, keepdims=True)
        s1 = jnp.sum(jnp.where(first, 0.0, xp), axis=-1, keepdims=True)
        outs.append(jnp.where(first, s0, s1))
    return jnp.concatenate(outs, axis=1)


def _ffn_half_step(x, gain_ref, wg_ref, wu_ref, wd_ref):
    h = _rms(x, gain_ref[...]).astype(BF16)
    gate = jnp.dot(h, wg_ref[...], preferred_element_type=F32)
    up = jnp.dot(h, wu_ref[...], preferred_element_type=F32)
    act = (gate * _sigmoid(gate) * up).astype(BF16)
    return x + MACARON_WEIGHT * jnp.dot(act, wd_ref[...], preferred_element_type=F32)


def _ffn_kernel(x_ref, gain_ref, wg_ref, wu_ref, wd_ref, o_ref):
    o_ref[...] = _ffn_half_step(x_ref[...], gain_ref, wg_ref, wu_ref, wd_ref)


def _const_spec(shape):
    return pl.BlockSpec(shape, lambda *_: (0,) * len(shape), pipeline_mode=pl.Buffered(1))


def _ffn(x2d, gain, wg, wu, wd, *, row_tile):
    rows, d = x2d.shape
    dff = wg.shape[1]
    return pl.pallas_call(
        _ffn_kernel,
        out_shape=jax.ShapeDtypeStruct((rows, d), F32),
        grid=(rows // row_tile,),
        in_specs=[
            pl.BlockSpec((row_tile, d), lambda i: (i, 0)),
            _const_spec((1, d)),
            _const_spec((d, dff)),
            _const_spec((d, dff)),
            _const_spec((dff, d)),
        ],
        out_specs=pl.BlockSpec((row_tile, d), lambda i: (i, 0)),
        compiler_params=pltpu.CompilerParams(
            dimension_semantics=("arbitrary",), vmem_limit_bytes=VMEM_LIMIT_BYTES),
        name="ffn",
    )(x2d, gain, wg, wu, wd)


def _chunk_constants():
    idx = np.arange(CHUNK)
    def tile(m):
        return np.tile(m.astype(np.float32), (1, GROUP))
    masks = [tile(idx[None, :] < idx[:, None])]
    for l in range(1, N_LEVELS + 1):
        same = (idx[:, None] >> l) == (idx[None, :] >> l)
        cross = ((idx[:, None] >> (l - 1)) & 1 == 1) & ((idx[None, :] >> (l - 1)) & 1 == 0)
        masks.append(tile(same & cross))
    masks.append(tile(idx[None, :] <= idx[:, None]))
    masks.append(tile(idx[None, :] == idx[:, None]))
    masks = np.stack(masks)
    ltri = (idx[None, :] <= idx[:, None]).astype(np.float32)
    mats = [ltri]
    for l in range(1, N_LEVELS + 1):
        hblk = idx >> (l - 1)
        half = hblk & 1
        same_h = hblk[:, None] == hblk[None, :]
        eq = same_h & (idx[None, :] <= idx[:, None]) & (half[:, None] == 1)
        ek = same_h & (idx[None, :] > idx[:, None]) & (half[:, None] == 0)
        mats.append((eq | ek).astype(np.float32))
    emat = np.concatenate(mats, axis=0)
    return masks, np.concatenate([emat, emat], axis=1), np.concatenate([ltri, ltri], axis=1)


def _mixer_ffn_kernel(x_ref, mixg_ref, win_ref, aw2_ref, ab_ref, gnorm_ref, mu_ref, w0_ref, w2_ref,
                      a0_ref, a2_ref, g2_ref, kk_ref, ka_ref, rk_ref, lnw_ref, lnb_ref, wout_ref,
                      mask_ref, emat_ref, ltri_ref, f2g_ref, f2wg_ref, f2wu_ref, f2wd_ref, fgain_ref,
                      o_ref,
                      proj_ref, carry_ref, r_ref, k_ref, v_ref, lw_ref, av_ref, bv_ref, la_ref,
                      mix_ref, gst_ref, rst_ref, mixed_x_ref, ffn_h_ref, ffn_acc_ref, *, tile, tiles_per_seq):
    step = pl.program_id(0)

    @pl.when(step == 0)
    def _():
        mixed_x_ref[...] = jnp.zeros_like(mixed_x_ref)

    @pl.when(step % tiles_per_seq == 0)
    def _():
        carry_ref[...] = jnp.zeros_like(carry_ref)
        gst_ref[...] = jnp.zeros_like(gst_ref)
        rst_ref[...] = jnp.zeros_like(rst_ref)

    def ffn_pieces():
        ffn_h_ref[...] = _rms(mixed_x_ref[...], f2g_ref[...]).astype(BF16)
        for c in range(f2wg_ref.shape[1] // FFN_COLS):
            cols = slice(c * FFN_COLS, (c + 1) * FFN_COLS)
            gate = jnp.dot(ffn_h_ref[...], f2wg_ref[:, cols], preferred_element_type=F32)
            yield
            up = jnp.dot(ffn_h_ref[...], f2wu_ref[:, cols], preferred_element_type=F32)
            yield
            act = (gate * _sigmoid(gate) * up).astype(BF16)
            down = jnp.dot(act, f2wd_ref[cols, :], preferred_element_type=F32)
            ffn_acc_ref[...] = down if c == 0 else ffn_acc_ref[...] + down
            yield
        o_ref[...] = _rms(mixed_x_ref[...] + MACARON_WEIGHT * ffn_acc_ref[...], fgain_ref[...])

    pieces = ffn_pieces()

    def ffn_tick(n=1):
        for _ in range(n):
            next(pieces, None)

    x = x_ref[...]
    h = _rms(x, mixg_ref[...]).astype(BF16)
    proj_ref[...] = jnp.dot(h, win_ref[...], preferred_element_type=F32)
    ffn_tick(3)

    z = _dot(proj_ref[:, OFF_ALR:OFF_ALR + LANES], aw2_ref[...]) + ab_ref[...]
    la_ref[...] = (jnp.minimum(z, 0.0) - _log1pexp_neg_abs(z)) * (1.0 / GLA_GATE_NORMALIZER)

    p = proj_ref[:, OFF_RW:OFF_RW + RW_COLS]
    row = lax.broadcasted_iota(jnp.int32, p.shape, 0)
    prev = jnp.where(row == 0, carry_ref[...], pltpu.roll(p, 1, axis=0))
    carry_ref[...] = p[tile - 1:tile, :]
    p = p + mu_ref[...] * (prev - p)
    r = p[:, RW_R:RW_R + RWKV_W]
    k = p[:, RW_K:RW_K + RWKV_W]
    v = p[:, RW_V:RW_V + RWKV_W]
    lr = p[:, RW_LR:RW_LR + LANES]
    u = w0_ref[...] + _dot(jnp.tanh(lr), w2_ref[...])
    w = jnp.minimum(u, 0.0) - _log1pexp_neg_abs(u) - 0.5
    a = _sigmoid(a0_ref[...] + _dot(lr, a2_ref[...]))
    kk = k * kk_ref[...]
    kk = kk / jnp.maximum(jnp.sqrt(_pair_sums(kk * kk)), 1e-12)
    k = k * (1.0 + (a - 1.0) * ka_ref[...])
    r_ref[...] = r
    k_ref[...] = k
    v_ref[...] = v
    lw_ref[...] = -jnp.exp(w)
    av_ref[...] = -kk
    bv_ref[...] = kk * a
    bonus = _pair_sums(r * k * rk_ref[...]) * v
    gate = _dot(_sigmoid(p[:, RW_G:RW_G + LANES]), g2_ref[...])
    ffn_tick(3)

    def mask(i):
        return mask_ref[i] > 0.0

    n_chunks = tile // CHUNK
    chunk_rows = [slice(c * CHUNK, (c + 1) * CHUNK) for c in range(n_chunks)]

    gla = []
    for rows in chunk_rows:
        d = {}
        d["q"] = proj_ref[rows, OFF_Q:OFF_Q + GLA_QK] * (GLA_DK ** -0.5)
        d["k"] = proj_ref[rows, OFF_K:OFF_K + GLA_QK]
        d["v"] = proj_ref[rows, OFF_V:OFF_V + GLA_V]
        d["eg"] = _split_dot(emat_ref[...], la_ref[rows, :])
        gla.append(d)
    ffn_tick()
    for d in gla:
        d["scores"] = jnp.where(mask(8), _dot_nt(d["q"], _bd_rows(d["k"], GLA_DK)), 0.0)
    for l in range(1, N_LEVELS + 1):
        for d in gla:
            e_l = jnp.exp(d["eg"][CHUNK * l:CHUNK * (l + 1)])
            d["scores"] = d["scores"] + jnp.where(mask(l), _dot_nt(d["q"] * e_l, _bd_rows(d["k"] * e_l, GLA_DK)), 0.0)
        ffn_tick()
    for d in gla:
        cum = d["eg"][0:CHUNK]
        last = cum[CHUNK - 1:CHUNK, :]
        d["o"] = _dot(d["scores"], _bd_rows(d["v"], GLA_DV))
        d["qe"] = d["q"] * jnp.exp(cum)
        kd = d["k"] * jnp.exp(last - cum)
        stack_t = jnp.concatenate([kd, jnp.broadcast_to(jnp.exp(last), kd.shape)], axis=0).T
        d["decay_col"] = stack_t[:, CHUNK:CHUNK + 1]
        vpad = jnp.concatenate([d["v"], jnp.zeros_like(d["v"])], axis=0)
        kv = []
        for pr in range(GLA_HEADS // 2):
            both = _dot(stack_t[pr * 2 * GLA_DK:(pr + 1) * 2 * GLA_DK, :],
                        vpad[:, pr * 2 * GLA_DV:(pr + 1) * 2 * GLA_DV])
            kv += [both[0:GLA_DK, 0:GLA_DV], both[GLA_DK:, GLA_DV:]]
        d["kv"] = jnp.concatenate(kv, axis=0)
    ffn_tick()
    gst = gst_ref[...]
    for rows, d in zip(chunk_rows, gla):
        o = d["o"] + _dot(d["qe"], _stack_to_bd(gst, GLA_DK))
        gst = gst * d["decay_col"] + d["kv"]
        for hd in range(GLA_HEADS):
            sl = slice(hd * GLA_DV, (hd + 1) * GLA_DV)
            oh = o[:, sl]
            oh = oh * lax.rsqrt(jnp.mean(oh * oh, axis=-1, keepdims=True) + NORM_EPS) * gnorm_ref[...]
            go = proj_ref[rows, OFF_GO + hd * GLA_DV:OFF_GO + (hd + 1) * GLA_DV]
            mix_ref[rows, sl] = oh * (go * _sigmoid(go))
    gst_ref[...] = gst
    ffn_tick()

    n_groups = RWKV_HEADS // GROUP
    gw = GROUP * RWKV_N
    items = []
    for rows in chunk_rows:
        cum_all = _split_dot(ltri_ref[...], lw_ref[rows, :])
        for g in range(n_groups):
            gl = slice(g * gw, (g + 1) * gw)
            rc, kc, vc = r_ref[rows, gl], k_ref[rows, gl], v_ref[rows, gl]
            lw, ac, bc = lw_ref[rows, gl], av_ref[rows, gl], bv_ref[rows, gl]
            cum = cum_all[:, gl]
            last = cum[CHUNK - 1:CHUNK, :]
            e_neg = jnp.exp(-cum)
            e_last = jnp.exp(last - cum)
            d = {"rows": rows, "g": g, "v": vc, "e_last_row": jnp.exp(last)}
            d["at"] = ac * jnp.exp(cum - lw)
            d["rt"] = rc * jnp.exp(cum)
            d["bt_bd"] = _bd_rows(bc * e_neg, RWKV_N)
            d["kt_bd"] = _bd_rows(kc * e_neg, RWKV_N)
            d["bk_t"] = jnp.concatenate([bc * e_last, kc * e_last], axis=0).T
            d["vbd"] = _bd_rows(vc, RWKV_N)
            items.append(d)
    ffn_tick()
    for d in items:
        ar = jnp.concatenate([d["at"], d["rt"]], axis=0)
        ab_ = _dot_nt(ar, d["bt_bd"])
        ak_ = _dot_nt(ar, d["kt_bd"])
        d["a_ab"] = jnp.where(mask(0), ab_[0:CHUNK], 0.0)
        d["a_ak"] = jnp.where(mask(0), ak_[0:CHUNK], 0.0)
        d["a_rb"] = jnp.where(mask(7), ab_[CHUNK:], 0.0)
        d["a_rk"] = jnp.where(mask(7), ak_[CHUNK:], 0.0)
        d["tinv"] = jnp.where(mask(8), 1.0, 0.0) + jnp.where(mask(1), d["a_ab"], 0.0)
    ffn_tick()
    for d in items:
        d["akv"] = _dot(d["a_ak"], d["vbd"])
    ffn_tick()
    for l in range(2, N_LEVELS + 1):
        for d in items:
            d["x"] = _dot(jnp.where(mask(l), d["a_ab"], 0.0), _bd_rows(d["tinv"], CHUNK))
        ffn_tick()
        for d in items:
            d["tinv"] = d["tinv"] + _dot(d["tinv"], _bd_rows(d["x"], CHUNK))
        ffn_tick()
    for d in items:
        tw = _dot(d["tinv"], jnp.concatenate([_bd_rows(d["at"], RWKV_N), _bd_rows(d["akv"], RWKV_N)], axis=1))
        d["ap"], d["uv"] = tw[:, 0:gw], tw[:, gw:]
    ffn_tick()
    for d in items:
        d["rp"] = d["rt"] + _dot(d["a_rb"], _bd_rows(d["ap"], RWKV_N))
        d["yv"] = _dot(jnp.concatenate([d["a_rb"], d["a_rk"]], axis=1),
                       jnp.concatenate([_bd_rows(d["uv"], RWKV_N), d["vbd"]], axis=0))
    ffn_tick()
    rowi = lax.broadcasted_iota(jnp.int32, (gw, gw), 0)
    coli = lax.broadcasted_iota(jnp.int32, (gw, gw), 1)
    for d in items:
        m_tiles, n_tiles = [], []
        for pr in range(GROUP // 2):
            pl_ = slice(pr * LANES, (pr + 1) * LANES)
            rhs = jnp.concatenate(
                [jnp.concatenate([d["ap"][:, pl_], d["uv"][:, pl_]], axis=1),
                 jnp.concatenate([jnp.zeros_like(d["v"][:, pl_]), d["v"][:, pl_]], axis=1)], axis=0)
            both = _dot(d["bk_t"][pr * LANES:(pr + 1) * LANES, :], rhs)
            m_tiles.append(both[:, 0:LANES])
            n_tiles.append(both[:, LANES:])
        d["mc"] = _pair_blocks_to_bd(m_tiles) + jnp.where(rowi == coli, d["e_last_row"], 0.0)
        d["nc"] = _pair_blocks_to_bd(n_tiles)
    ffn_tick()
    states = [rst_ref[g] for g in range(n_groups)]
    for d in items:
        g = d["g"]
        mix_ref[d["rows"], GLA_V + g * gw:GLA_V + (g + 1) * gw] = _dot(d["rp"], states[g]) + d["yv"]
        states[g] = _dot(d["mc"], states[g]) + d["nc"]
        ffn_tick()
    for g in range(n_groups):
        rst_ref[g] = states[g]

    y = mix_ref[:, GLA_V:]
    mean = _pair_sums(y) * (1.0 / RWKV_N)
    yc = y - mean
    var = _pair_sums(yc * yc) * (1.0 / RWKV_N)
    y = yc * lax.rsqrt(var + RWKV_GN_EPS) * lnw_ref[...] + lnb_ref[...]
    mix_ref[:, GLA_V:] = (y + bonus) * gate
    ffn_tick(3 * (f2wg_ref.shape[1] // FFN_COLS) + 1)
    mixed_x_ref[...] = x + jnp.dot(mix_ref[...].astype(BF16), wout_ref[...], preferred_element_type=F32)


def _mixer_ffn(x2d, seq, mixer_params, ffn_params, *, tile):
    rows, d = x2d.shape
    n_tiles = rows // tile
    masks, emat, ltri = _chunk_constants()
    consts = [jnp.asarray(masks), jnp.asarray(emat, BF16), jnp.asarray(ltri, BF16)]
    resident = list(mixer_params) + consts + list(ffn_params)
    return pl.pallas_call(
        functools.partial(_mixer_ffn_kernel, tile=tile, tiles_per_seq=seq // tile),
        out_shape=jax.ShapeDtypeStruct((rows, d), F32),
        grid=(n_tiles + 1,),
        in_specs=[pl.BlockSpec((tile, d), lambda s: (jnp.minimum(s, n_tiles - 1), 0))]
        + [_const_spec(a.shape) for a in resident],
        out_specs=pl.BlockSpec((tile, d), lambda s: (jnp.maximum(s - 1, 0), 0)),
        scratch_shapes=[
            pltpu.VMEM((tile, PROJ_PAD), F32),
            pltpu.VMEM((1, RW_COLS), F32),
            pltpu.VMEM((tile, RWKV_W), F32),
            pltpu.VMEM((tile, RWKV_W), F32),
            pltpu.VMEM((tile, RWKV_W), F32),
            pltpu.VMEM((tile, RWKV_W), F32),
            pltpu.VMEM((tile, RWKV_W), F32),
            pltpu.VMEM((tile, RWKV_W), F32),
            pltpu.VMEM((tile, GLA_QK), F32),
            pltpu.VMEM((tile, GLA_V + RWKV_W), F32),
            pltpu.VMEM((GLA_HEADS * GLA_DK, GLA_DV), F32),
            pltpu.VMEM((RWKV_HEADS // GROUP, GROUP * RWKV_N, GROUP * RWKV_N), F32),
            pltpu.VMEM((tile, d), F32),
            pltpu.VMEM((tile, d), BF16),
            pltpu.VMEM((tile, d), F32),
        ],
        compiler_params=pltpu.CompilerParams(
            dimension_semantics=("arbitrary",), vmem_limit_bytes=VMEM_LIMIT_BYTES),
        name="mixer_ffn",
    )(x2d, *resident)


def _row(vec):
    return vec.reshape(1, -1).astype(F32)


def kernel(x, ffn1_norm, ffn1_w_gate, ffn1_w_up, ffn1_w_down, mix_norm, w_in, gla_alpha_w2, gla_alpha_b, gla_norm, rwkv_mu, rwkv_w0, rwkv_w2, rwkv_a0, rwkv_a2, rwkv_g2, rwkv_k_k, rwkv_k_a, rwkv_r_k, rwkv_ln_w, rwkv_ln_b, w_out, ffn2_norm, ffn2_w_gate, ffn2_w_up, ffn2_w_down, final_norm):
    bsz, seq, d = x.shape
    depth = ffn1_norm.shape[0]
    assert depth == 1, "the fused mixer + FFN + final-norm call is written for the single-layer block"
    rows = bsz * seq
    row_tile = min(512, rows)
    tile = min(256, seq)
    gla_cols = OFF_ALR + GLA_RANK
    l = 0
    x2 = _ffn(x.reshape(rows, d), _row(ffn1_norm[l]), ffn1_w_gate[l].astype(BF16),
              ffn1_w_up[l].astype(BF16), ffn1_w_down[l].astype(BF16), row_tile=row_tile)
    win = jnp.concatenate([w_in[l][:, :gla_cols], jnp.zeros((d, LANES - GLA_RANK), F32),
                           w_in[l][:, gla_cols:]], axis=1).astype(BF16)
    aw2 = jnp.concatenate([gla_alpha_w2[l], jnp.zeros((LANES - GLA_RANK, GLA_QK), F32)], axis=0).astype(BF16)
    zeros64 = jnp.zeros((64, RWKV_W), F32)
    w2 = jnp.concatenate([rwkv_w2[l], zeros64], axis=0).astype(BF16)
    a2 = jnp.concatenate([zeros64, rwkv_a2[l]], axis=0).astype(BF16)
    mixer_params = [_row(mix_norm[l]), win, aw2, _row(gla_alpha_b[l]), _row(gla_norm[l]),
                    _row(rwkv_mu[l]), _row(rwkv_w0[l]), w2, _row(rwkv_a0[l]), a2, rwkv_g2[l].astype(BF16),
                    _row(rwkv_k_k[l]), _row(rwkv_k_a[l]), _row(rwkv_r_k[l]), _row(rwkv_ln_w[l]),
                    _row(rwkv_ln_b[l]), w_out[l].astype(BF16)]
    ffn_params = [_row(ffn2_norm[l]), ffn2_w_gate[l].astype(BF16), ffn2_w_up[l].astype(BF16),
                  ffn2_w_down[l].astype(BF16), _row(final_norm)]
    out = _mixer_ffn(x2, seq, mixer_params, ffn_params, tile=tile)
    return out.reshape(bsz, seq, d)
```

```python
import functools

import numpy as np
import jax
import jax.numpy as jnp
from jax import lax
from jax.experimental import pallas as pl
from jax.experimental.pallas import tpu as pltpu

F32 = jnp.float32
BF16 = jnp.bfloat16

NORM_EPS = 1e-6
MACARON_WEIGHT = 0.5
CHUNK = 64
GROUP = 4
LANES = 128
GLA_HEADS = 4
GLA_DK = 64
GLA_DV = 128
GLA_RANK = 16
GLA_GATE_NORMALIZER = 16.0
RWKV_HEADS = 8
RWKV_N = 64
RWKV_GN_EPS = 64e-5
GLA_QK = GLA_HEADS * GLA_DK
GLA_V = GLA_HEADS * GLA_DV
RWKV_W = RWKV_HEADS * RWKV_N
OFF_Q = 0
OFF_K = OFF_Q + GLA_QK
OFF_V = OFF_K + GLA_QK
OFF_GO = OFF_V + GLA_V
OFF_ALR = OFF_GO + GLA_V
OFF_RW = OFF_ALR + LANES
RW_COLS = 3 * RWKV_W + 64 + 64 + 128
RW_R, RW_K, RW_V, RW_LR, RW_G = 0, RWKV_W, 2 * RWKV_W, 3 * RWKV_W, 3 * RWKV_W + LANES
PROJ_PAD = OFF_RW + RW_COLS
N_LEVELS = 6
MIX_TILE = 512
FFN_ROWS = 512
VMEM_LIMIT_BYTES = 56 * 1024 * 1024


def _dot(a, b):
    return jnp.dot(a.astype(BF16), b.astype(BF16), preferred_element_type=F32)


def _dot_nt(a, b):
    return lax.dot_general(a.astype(BF16), b.astype(BF16), (((1,), (1,)), ((), ())),
                           preferred_element_type=F32)


def _split_dot(m, x):
    hi = x.astype(BF16)
    lo = (x - hi.astype(F32)).astype(BF16)
    return jnp.dot(m, jnp.concatenate([hi, lo], axis=0), preferred_element_type=F32)


def _sigmoid(x):
    return 1.0 / (1.0 + jnp.exp(-x))


def _log1pexp_neg_abs(x):
    return jnp.log(1.0 + jnp.exp(-jnp.abs(x)))


def _rms(x, gain):
    return x * lax.rsqrt(jnp.mean(x * x, axis=-1, keepdims=True) + NORM_EPS) * gain


def _bd_rows(y, width):
    n_tiles = y.shape[1] // LANES
    lane = lax.broadcasted_iota(jnp.int32, (y.shape[0], LANES), 1)
    rows = []
    for h in range(GROUP):
        lo, hi = h * width, (h + 1) * width
        tiles = []
        for j in range(n_tiles):
            tl, th = j * LANES, (j + 1) * LANES
            tile = y[:, tl:th]
            if hi <= tl or lo >= th:
                tiles.append(jnp.zeros_like(tile))
            elif lo <= tl and hi >= th:
                tiles.append(tile)
            elif lo <= tl:
                tiles.append(jnp.where(lane < hi - tl, tile, 0.0))
            else:
                tiles.append(jnp.where(lane >= lo - tl, tile, 0.0))
        rows.append(jnp.concatenate(tiles, axis=1) if n_tiles > 1 else tiles[0])
    return jnp.concatenate(rows, axis=0)


def _stack_to_bd(s, rows_per_head):
    rows = []
    for h in range(GROUP):
        blk = s[h * rows_per_head:(h + 1) * rows_per_head, :]
        rows.append(jnp.concatenate([blk if j == h else jnp.zeros_like(blk) for j in range(GROUP)], axis=1))
    return jnp.concatenate(rows, axis=0)


def _pair_blocks_to_bd(tiles):
    half = tiles[0].shape[0] // 2
    lane = lax.broadcasted_iota(jnp.int32, (half, LANES), 1)
    rows = []
    for p, t in enumerate(tiles):
        kept = jnp.concatenate([jnp.where(lane < half, t[0:half], 0.0),
                                jnp.where(lane >= half, t[half:], 0.0)], axis=0)
        rows.append(jnp.concatenate([kept if j == p else jnp.zeros_like(kept) for j in range(len(tiles))], axis=1))
    return jnp.concatenate(rows, axis=0)


def _pair_sums(x):
    lane = lax.broadcasted_iota(jnp.int32, (x.shape[0], LANES), 1)
    first = lane < RWKV_N
    outs = []
    for p in range(x.shape[1] // LANES):
        xp = x[:, p * LANES:(p + 1) * LANES]
        s0 = jnp.sum(jnp.where(first, xp, 0.0), axis=-1, keepdims=True)
        s1 = jnp.sum(jnp.where(first, 0.0, xp), axis=-1, keepdims=True)
        outs.append(jnp.where(first, s0, s1))
    return jnp.concatenate(outs, axis=1)


def _ffn_half_step(x, gain_ref, wg_ref, wu_ref, wd_ref):
    h = _rms(x, gain_ref[...]).astype(BF16)
    gate = jnp.dot(h, wg_ref[...], preferred_element_type=F32)
    up = jnp.dot(h, wu_ref[...], preferred_element_type=F32)
    act = (gate * _sigmoid(gate) * up).astype(BF16)
    return x + MACARON_WEIGHT * jnp.dot(act, wd_ref[...], preferred_element_type=F32)


def _ffn_kernel(x_ref, gain_ref, wg_ref, wu_ref, wd_ref, o_ref):
    o_ref[...] = _ffn_half_step(x_ref[...], gain_ref, wg_ref, wu_ref, wd_ref)


def _ffn_final_kernel(x_ref, gain_ref, wg_ref, wu_ref, wd_ref, fgain_ref, o_ref):
    o_ref[...] = _rms(_ffn_half_step(x_ref[...], gain_ref, wg_ref, wu_ref, wd_ref), fgain_ref[...])


def _const_spec(shape):
    return pl.BlockSpec(shape, lambda *_: (0,) * len(shape), pipeline_mode=pl.Buffered(1))


def _ffn(x2d, gain, wg, wu, wd, final_gain=None, *, row_tile):
    rows, d = x2d.shape
    operands = [x2d, gain, wg, wu, wd] + ([] if final_gain is None else [final_gain])
    return pl.pallas_call(
        _ffn_kernel if final_gain is None else _ffn_final_kernel,
        out_shape=jax.ShapeDtypeStruct((rows, d), F32),
        grid=(rows // row_tile,),
        in_specs=[pl.BlockSpec((row_tile, d), lambda i: (i, 0))] + [_const_spec(a.shape) for a in operands[1:]],
        out_specs=pl.BlockSpec((row_tile, d), lambda i: (i, 0)),
        compiler_params=pltpu.CompilerParams(
            dimension_semantics=("arbitrary",), vmem_limit_bytes=VMEM_LIMIT_BYTES),
        name="ffn" if final_gain is None else "ffn_final",
    )(*operands)


def _chunk_constants():
    idx = np.arange(CHUNK)
    def tile(m):
        return np.tile(m.astype(np.float32), (1, GROUP))
    masks = [tile(idx[None, :] < idx[:, None])]
    for l in range(1, N_LEVELS + 1):
        same = (idx[:, None] >> l) == (idx[None, :] >> l)
        cross = ((idx[:, None] >> (l - 1)) & 1 == 1) & ((idx[None, :] >> (l - 1)) & 1 == 0)
        masks.append(tile(same & cross))
    masks.append(tile(idx[None, :] <= idx[:, None]))
    masks.append(tile(idx[None, :] == idx[:, None]))
    masks = np.stack(masks)
    ltri = (idx[None, :] <= idx[:, None]).astype(np.float32)
    mats = [ltri]
    for l in range(1, N_LEVELS + 1):
        hblk = idx >> (l - 1)
        half = hblk & 1
        same_h = hblk[:, None] == hblk[None, :]
        eq = same_h & (idx[None, :] <= idx[:, None]) & (half[:, None] == 1)
        ek = same_h & (idx[None, :] > idx[:, None]) & (half[:, None] == 0)
        mats.append((eq | ek).astype(np.float32))
    emat = np.concatenate(mats, axis=0)
    return masks, np.concatenate([emat, emat], axis=1), np.concatenate([ltri, ltri], axis=1)


def _mixer_kernel(x_ref, mixg_ref, win_ref, aw2_ref, ab_ref, gnorm_ref, mu_ref, w0_ref, w2_ref,
                  a0_ref, a2_ref, g2_ref, kk_ref, ka_ref, rk_ref, lnw_ref, lnb_ref, wout_ref,
                  mask_ref, emat_ref, ltri_ref,
                  o_ref,
                  proj_ref, carry_ref, r_ref, k_ref, v_ref, lw_ref, av_ref, bv_ref, la_ref,
                  mix_ref, gst_ref, rst_ref, *, tile, tiles_per_seq):
    step = pl.program_id(0)

    @pl.when(step % tiles_per_seq == 0)
    def _():
        carry_ref[...] = jnp.zeros_like(carry_ref)
        gst_ref[...] = jnp.zeros_like(gst_ref)
        rst_ref[...] = jnp.zeros_like(rst_ref)

    x = x_ref[...]
    h = _rms(x, mixg_ref[...]).astype(BF16)
    proj_ref[...] = jnp.dot(h, win_ref[...], preferred_element_type=F32)

    z = _dot(proj_ref[:, OFF_ALR:OFF_ALR + LANES], aw2_ref[...]) + ab_ref[...]
    la_ref[...] = (jnp.minimum(z, 0.0) - _log1pexp_neg_abs(z)) * (1.0 / GLA_GATE_NORMALIZER)

    p = proj_ref[:, OFF_RW:OFF_RW + RW_COLS]
    row = lax.broadcasted_iota(jnp.int32, p.shape, 0)
    prev = jnp.where(row == 0, carry_ref[...], pltpu.roll(p, 1, axis=0))
    carry_ref[...] = p[tile - 1:tile, :]
    p = p + mu_ref[...] * (prev - p)
    r = p[:, RW_R:RW_R + RWKV_W]
    k = p[:, RW_K:RW_K + RWKV_W]
    v = p[:, RW_V:RW_V + RWKV_W]
    lr = p[:, RW_LR:RW_LR + LANES]
    u = w0_ref[...] + _dot(jnp.tanh(lr), w2_ref[...])
    w = jnp.minimum(u, 0.0) - _log1pexp_neg_abs(u) - 0.5
    a = _sigmoid(a0_ref[...] + _dot(lr, a2_ref[...]))
    kk = k * kk_ref[...]
    kk = kk / jnp.maximum(jnp.sqrt(_pair_sums(kk * kk)), 1e-12)
    k = k * (1.0 + (a - 1.0) * ka_ref[...])
    r_ref[...] = r
    k_ref[...] = k
    v_ref[...] = v
    lw_ref[...] = -jnp.exp(w)
    av_ref[...] = -kk
    bv_ref[...] = kk * a
    bonus = _pair_sums(r * k * rk_ref[...]) * v
    gate = _dot(_sigmoid(p[:, RW_G:RW_G + LANES]), g2_ref[...])

    def mask(i):
        return mask_ref[i] > 0.0

    n_chunks = tile // CHUNK
    chunk_rows = [slice(c * CHUNK, (c + 1) * CHUNK) for c in range(n_chunks)]

    gla = []
    for rows in chunk_rows:
        d = {}
        d["q"] = proj_ref[rows, OFF_Q:OFF_Q + GLA_QK] * (GLA_DK ** -0.5)
        d["k"] = proj_ref[rows, OFF_K:OFF_K + GLA_QK]
        d["v"] = proj_ref[rows, OFF_V:OFF_V + GLA_V]
        d["eg"] = _split_dot(emat_ref[...], la_ref[rows, :])
        gla.append(d)
    for d in gla:
        d["scores"] = jnp.where(mask(8), _dot_nt(d["q"], _bd_rows(d["k"], GLA_DK)), 0.0)
    for l in range(1, N_LEVELS + 1):
        for d in gla:
            e_l = jnp.exp(d["eg"][CHUNK * l:CHUNK * (l + 1)])
            d["scores"] = d["scores"] + jnp.where(mask(l), _dot_nt(d["q"] * e_l, _bd_rows(d["k"] * e_l, GLA_DK)), 0.0)
    for d in gla:
        cum = d["eg"][0:CHUNK]
        last = cum[CHUNK - 1:CHUNK, :]
        d["o"] = _dot(d["scores"], _bd_rows(d["v"], GLA_DV))
        d["qe"] = d["q"] * jnp.exp(cum)
        kd = d["k"] * jnp.exp(last - cum)
        stack_t = jnp.concatenate([kd, jnp.broadcast_to(jnp.exp(last), kd.shape)], axis=0).T
        d["decay_col"] = stack_t[:, CHUNK:CHUNK + 1]
        vpad = jnp.concatenate([d["v"], jnp.zeros_like(d["v"])], axis=0)
        kv = []
        for pr in range(GLA_HEADS // 2):
            both = _dot(stack_t[pr * 2 * GLA_DK:(pr + 1) * 2 * GLA_DK, :],
                        vpad[:, pr * 2 * GLA_DV:(pr + 1) * 2 * GLA_DV])
            kv += [both[0:GLA_DK, 0:GLA_DV], both[GLA_DK:, GLA_DV:]]
        d["kv"] = jnp.concatenate(kv, axis=0)
    gst = gst_ref[...]
    for rows, d in zip(chunk_rows, gla):
        o = d["o"] + _dot(d["qe"], _stack_to_bd(gst, GLA_DK))
        gst = gst * d["decay_col"] + d["kv"]
        for hd in range(GLA_HEADS):
            sl = slice(hd * GLA_DV, (hd + 1) * GLA_DV)
            oh = o[:, sl]
            oh = oh * lax.rsqrt(jnp.mean(oh * oh, axis=-1, keepdims=True) + NORM_EPS) * gnorm_ref[...]
            go = proj_ref[rows, OFF_GO + hd * GLA_DV:OFF_GO + (hd + 1) * GLA_DV]
            mix_ref[rows, sl] = oh * (go * _sigmoid(go))
    gst_ref[...] = gst

    n_groups = RWKV_HEADS // GROUP
    gw = GROUP * RWKV_N
    items = []
    for rows in chunk_rows:
        cum_all = _split_dot(ltri_ref[...], lw_ref[rows, :])
        for g in range(n_groups):
            gl = slice(g * gw, (g + 1) * gw)
            rc, kc, vc = r_ref[rows, gl], k_ref[rows, gl], v_ref[rows, gl]
            lw, ac, bc = lw_ref[rows, gl], av_ref[rows, gl], bv_ref[rows, gl]
            cum = cum_all[:, gl]
            last = cum[CHUNK - 1:CHUNK, :]
            e_neg = jnp.exp(-cum)
            e_last = jnp.exp(last - cum)
            d = {"rows": rows, "g": g, "v": vc, "e_last_row": jnp.exp(last)}
            d["at"] = ac * jnp.exp(cum - lw)
            d["rt"] = rc * jnp.exp(cum)
            d["bt_bd"] = _bd_rows(bc * e_neg, RWKV_N)
            d["kt_bd"] = _bd_rows(kc * e_neg, RWKV_N)
            d["bk_t"] = jnp.concatenate([bc * e_last, kc * e_last], axis=0).T
            d["vbd"] = _bd_rows(vc, RWKV_N)
            items.append(d)
    for d in items:
        ar = jnp.concatenate([d["at"], d["rt"]], axis=0)
        ab_ = _dot_nt(ar, d["bt_bd"])
        ak_ = _dot_nt(ar, d["kt_bd"])
        d["a_ab"] = jnp.where(mask(0), ab_[0:CHUNK], 0.0)
        d["a_ak"] = jnp.where(mask(0), ak_[0:CHUNK], 0.0)
        d["a_rb"] = jnp.where(mask(7), ab_[CHUNK:], 0.0)
        d["a_rk"] = jnp.where(mask(7), ak_[CHUNK:], 0.0)
        d["tinv"] = jnp.where(mask(8), 1.0, 0.0) + jnp.where(mask(1), d["a_ab"], 0.0)
    for d in items:
        d["akv"] = _dot(d["a_ak"], d["vbd"])
    for l in range(2, N_LEVELS + 1):
        for d in items:
            d["x"] = _dot(jnp.where(mask(l), d["a_ab"], 0.0), _bd_rows(d["tinv"], CHUNK))
        for d in items:
            d["tinv"] = d["tinv"] + _dot(d["tinv"], _bd_rows(d["x"], CHUNK))
    for d in items:
        tw = _dot(d["tinv"], jnp.concatenate([_bd_rows(d["at"], RWKV_N), _bd_rows(d["akv"], RWKV_N)], axis=1))
        d["ap"], d["uv"] = tw[:, 0:gw], tw[:, gw:]
    for d in items:
        d["rp"] = d["rt"] + _dot(d["a_rb"], _bd_rows(d["ap"], RWKV_N))
        d["yv"] = _dot(jnp.concatenate([d["a_rb"], d["a_rk"]], axis=1),
                       jnp.concatenate([_bd_rows(d["uv"], RWKV_N), d["vbd"]], axis=0))
    rowi = lax.broadcasted_iota(jnp.int32, (gw, gw), 0)
    coli = lax.broadcasted_iota(jnp.int32, (gw, gw), 1)
    for d in items:
        m_tiles, n_tiles = [], []
        for pr in range(GROUP // 2):
            pl_ = slice(pr * LANES, (pr + 1) * LANES)
            rhs = jnp.concatenate(
                [jnp.concatenate([d["ap"][:, pl_], d["uv"][:, pl_]], axis=1),
                 jnp.concatenate([jnp.zeros_like(d["v"][:, pl_]), d["v"][:, pl_]], axis=1)], axis=0)
            both = _dot(d["bk_t"][pr * LANES:(pr + 1) * LANES, :], rhs)
            m_tiles.append(both[:, 0:LANES])
            n_tiles.append(both[:, LANES:])
        d["mc"] = _pair_blocks_to_bd(m_tiles) + jnp.where(rowi == coli, d["e_last_row"], 0.0)
        d["nc"] = _pair_blocks_to_bd(n_tiles)
    states = [rst_ref[g] for g in range(n_groups)]
    for d in items:
        g = d["g"]
        mix_ref[d["rows"], GLA_V + g * gw:GLA_V + (g + 1) * gw] = _dot(d["rp"], states[g]) + d["yv"]
        states[g] = _dot(d["mc"], states[g]) + d["nc"]
    for g in range(n_groups):
        rst_ref[g] = states[g]

    y = mix_ref[:, GLA_V:]
    mean = _pair_sums(y) * (1.0 / RWKV_N)
    yc = y - mean
    var = _pair_sums(yc * yc) * (1.0 / RWKV_N)
    y = yc * lax.rsqrt(var + RWKV_GN_EPS) * lnw_ref[...] + lnb_ref[...]
    mix_ref[:, GLA_V:] = (y + bonus) * gate
    o_ref[...] = x + jnp.dot(mix_ref[...].astype(BF16), wout_ref[...], preferred_element_type=F32)


def _mixer(x2d, seq, mixer_params, *, tile):
    rows, d = x2d.shape
    masks, emat, ltri = _chunk_constants()
    consts = [jnp.asarray(masks), jnp.asarray(emat, BF16), jnp.asarray(ltri, BF16)]
    resident = list(mixer_params) + consts
    return pl.pallas_call(
        functools.partial(_mixer_kernel, tile=tile, tiles_per_seq=seq // tile),
        out_shape=jax.ShapeDtypeStruct((rows, d), F32),
        grid=(rows // tile,),
        in_specs=[pl.BlockSpec((tile, d), lambda s: (s, 0))] + [_const_spec(a.shape) for a in resident],
        out_specs=pl.BlockSpec((tile, d), lambda s: (s, 0)),
        scratch_shapes=[
            pltpu.VMEM((tile, PROJ_PAD), F32),
            pltpu.VMEM((1, RW_COLS), F32),
            pltpu.VMEM((tile, RWKV_W), F32),
            pltpu.VMEM((tile, RWKV_W), F32),
            pltpu.VMEM((tile, RWKV_W), F32),
            pltpu.VMEM((tile, RWKV_W), F32),
            pltpu.VMEM((tile, RWKV_W), F32),
            pltpu.VMEM((tile, RWKV_W), F32),
            pltpu.VMEM((tile, GLA_QK), F32),
            pltpu.VMEM((tile, GLA_V + RWKV_W), F32),
            pltpu.VMEM((GLA_HEADS * GLA_DK, GLA_DV), F32),
            pltpu.VMEM((RWKV_HEADS // GROUP, GROUP * RWKV_N, GROUP * RWKV_N), F32),
        ],
        compiler_params=pltpu.CompilerParams(
            dimension_semantics=("arbitrary",), vmem_limit_bytes=VMEM_LIMIT_BYTES),
        name="mixer",
    )(x2d, *resident)


def _row(vec):
    return vec.reshape(1, -1).astype(F32)


def kernel(x, ffn1_norm, ffn1_w_gate, ffn1_w_up, ffn1_w_down, mix_norm, w_in, gla_alpha_w2, gla_alpha_b, gla_norm, rwkv_mu, rwkv_w0, rwkv_w2, rwkv_a0, rwkv_a2, rwkv_g2, rwkv_k_k, rwkv_k_a, rwkv_r_k, rwkv_ln_w, rwkv_ln_b, w_out, ffn2_norm, ffn2_w_gate, ffn2_w_up, ffn2_w_down, final_norm):
    bsz, seq, d = x.shape
    depth = ffn1_norm.shape[0]
    assert depth == 1, "the final norm is fused into the last FFN call, so one layer is assumed"
    rows = bsz * seq
    row_tile = min(FFN_ROWS, rows)
    tile = min(MIX_TILE, seq)
    gla_cols = OFF_ALR + GLA_RANK
    l = 0
    x2 = _ffn(x.reshape(rows, d), _row(ffn1_norm[l]), ffn1_w_gate[l].astype(BF16),
              ffn1_w_up[l].astype(BF16), ffn1_w_down[l].astype(BF16), row_tile=row_tile)
    win = jnp.concatenate([w_in[l][:, :gla_cols], jnp.zeros((d, LANES - GLA_RANK), F32),
                           w_in[l][:, gla_cols:]], axis=1).astype(BF16)
    aw2 = jnp.concatenate([gla_alpha_w2[l], jnp.zeros((LANES - GLA_RANK, GLA_QK), F32)], axis=0).astype(BF16)
    zeros64 = jnp.zeros((64, RWKV_W), F32)
    w2 = jnp.concatenate([rwkv_w2[l], zeros64], axis=0).astype(BF16)
    a2 = jnp.concatenate([zeros64, rwkv_a2[l]], axis=0).astype(BF16)
    mixer_params = [_row(mix_norm[l]), win, aw2, _row(gla_alpha_b[l]), _row(gla_norm[l]),
                    _row(rwkv_mu[l]), _row(rwkv_w0[l]), w2, _row(rwkv_a0[l]), a2, rwkv_g2[l].astype(BF16),
                    _row(rwkv_k_k[l]), _row(rwkv_k_a[l]), _row(rwkv_r_k[l]), _row(rwkv_ln_w[l]),
                    _row(rwkv_ln_b[l]), w_out[l].astype(BF16)]
    x3 = _mixer(x2, seq, mixer_params, tile=tile)
    out = _ffn(x3, _row(ffn2_norm[l]), ffn2_w_gate[l].astype(BF16), ffn2_w_up[l].astype(BF16),
               ffn2_w_down[l].astype(BF16), _row(final_norm), row_tile=row_tile)
    return out.reshape(bsz, seq, d)
```

```python
import functools

import numpy as np
import jax
import jax.numpy as jnp
from jax import lax
from jax.experimental import pallas as pl
from jax.experimental.pallas import tpu as pltpu

F32 = jnp.float32
BF16 = jnp.bfloat16

NORM_EPS = 1e-6
MACARON_WEIGHT = 0.5
CHUNK = 64
GROUP = 4
LANES = 128
GLA_HEADS = 4
GLA_DK = 64
GLA_DV = 128
GLA_RANK = 16
GLA_GATE_NORMALIZER = 16.0
RWKV_HEADS = 8
RWKV_N = 64
RWKV_GN_EPS = 64e-5
GLA_QK = GLA_HEADS * GLA_DK
GLA_V = GLA_HEADS * GLA_DV
RWKV_W = RWKV_HEADS * RWKV_N
OFF_Q = 0
OFF_K = OFF_Q + GLA_QK
OFF_V = OFF_K + GLA_QK
OFF_GO = OFF_V + GLA_V
OFF_ALR = OFF_GO + GLA_V
OFF_RW = OFF_ALR + LANES
RW_COLS = 3 * RWKV_W + 64 + 64 + 128
RW_R, RW_K, RW_V, RW_LR, RW_G = 0, RWKV_W, 2 * RWKV_W, 3 * RWKV_W, 3 * RWKV_W + LANES
PROJ_PAD = OFF_RW + RW_COLS
N_LEVELS = 6
MIX_TILE = 512
FFN_ROWS = 1024
VMEM_LIMIT_BYTES = 56 * 1024 * 1024


def _dot(a, b):
    return jnp.dot(a.astype(BF16), b.astype(BF16), preferred_element_type=F32)


def _dot_nt(a, b):
    return lax.dot_general(a.astype(BF16), b.astype(BF16), (((1,), (1,)), ((), ())),
                           preferred_element_type=F32)


def _split_dot(m, x):
    hi = x.astype(BF16)
    lo = (x - hi.astype(F32)).astype(BF16)
    return jnp.dot(m, jnp.concatenate([hi, lo], axis=0), preferred_element_type=F32)


def _sigmoid(x):
    return 1.0 / (1.0 + jnp.exp(-x))


def _log1pexp_neg_abs(x):
    return jnp.log(1.0 + jnp.exp(-jnp.abs(x)))


def _rms(x, gain):
    return x * lax.rsqrt(jnp.mean(x * x, axis=-1, keepdims=True) + NORM_EPS) * gain


def _bd_rows(y, width):
    n_tiles = y.shape[1] // LANES
    lane = lax.broadcasted_iota(jnp.int32, (y.shape[0], LANES), 1)
    rows = []
    for h in range(GROUP):
        lo, hi = h * width, (h + 1) * width
        tiles = []
        for j in range(n_tiles):
            tl, th = j * LANES, (j + 1) * LANES
            tile = y[:, tl:th]
            if hi <= tl or lo >= th:
                tiles.append(jnp.zeros_like(tile))
            elif lo <= tl and hi >= th:
                tiles.append(tile)
            elif lo <= tl:
                tiles.append(jnp.where(lane < hi - tl, tile, 0.0))
            else:
                tiles.append(jnp.where(lane >= lo - tl, tile, 0.0))
        rows.append(jnp.concatenate(tiles, axis=1) if n_tiles > 1 else tiles[0])
    return jnp.concatenate(rows, axis=0)


def _stack_to_bd(s, rows_per_head):
    rows = []
    for h in range(GROUP):
        blk = s[h * rows_per_head:(h + 1) * rows_per_head, :]
        rows.append(jnp.concatenate([blk if j == h else jnp.zeros_like(blk) for j in range(GROUP)], axis=1))
    return jnp.concatenate(rows, axis=0)


def _pair_blocks_to_bd(tiles):
    half = tiles[0].shape[0] // 2
    lane = lax.broadcasted_iota(jnp.int32, (half, LANES), 1)
    rows = []
    for p, t in enumerate(tiles):
        kept = jnp.concatenate([jnp.where(lane < half, t[0:half], 0.0),
                                jnp.where(lane >= half, t[half:], 0.0)], axis=0)
        rows.append(jnp.concatenate([kept if j == p else jnp.zeros_like(kept) for j in range(len(tiles))], axis=1))
    return jnp.concatenate(rows, axis=0)


def _pair_sums(x):
    lane = lax.broadcasted_iota(jnp.int32, (x.shape[0], LANES), 1)
    first = lane < RWKV_N
    outs = []
    for p in range(x.shape[1] // LANES):
        xp = x[:, p * LANES:(p + 1) * LANES]
        s0 = jnp.sum(jnp.where(first, xp, 0.0), axis=-1, keepdims=True)
        s1 = jnp.sum(jnp.where(first, 0.0, xp), axis=-1, keepdims=True)
        outs.append(jnp.where(first, s0, s1))
    return jnp.concatenate(outs, axis=1)


def _ffn_half_step(x, gain_ref, wg_ref, wu_ref, wd_ref):
    h = _rms(x, gain_ref[...]).astype(BF16)
    gate = jnp.dot(h, wg_ref[...], preferred_element_type=F32)
    up = jnp.dot(h, wu_ref[...], preferred_element_type=F32)
    act = (gate * _sigmoid(gate) * up).astype(BF16)
    return x + MACARON_WEIGHT * jnp.dot(act, wd_ref[...], preferred_element_type=F32)


def _ffn_kernel(x_ref, gain_ref, wg_ref, wu_ref, wd_ref, o_ref):
    o_ref[...] = _ffn_half_step(x_ref[...], gain_ref, wg_ref, wu_ref, wd_ref)


def _ffn_final_kernel(x_ref, gain_ref, wg_ref, wu_ref, wd_ref, fgain_ref, o_ref):
    o_ref[...] = _rms(_ffn_half_step(x_ref[...], gain_ref, wg_ref, wu_ref, wd_ref), fgain_ref[...])


def _const_spec(shape):
    return pl.BlockSpec(shape, lambda *_: (0,) * len(shape), pipeline_mode=pl.Buffered(1))


def _ffn(x2d, gain, wg, wu, wd, final_gain=None, *, row_tile):
    rows, d = x2d.shape
    operands = [x2d, gain, wg, wu, wd] + ([] if final_gain is None else [final_gain])
    return pl.pallas_call(
        _ffn_kernel if final_gain is None else _ffn_final_kernel,
        out_shape=jax.ShapeDtypeStruct((rows, d), F32),
        grid=(rows // row_tile,),
        in_specs=[pl.BlockSpec((row_tile, d), lambda i: (i, 0))] + [_const_spec(a.shape) for a in operands[1:]],
        out_specs=pl.BlockSpec((row_tile, d), lambda i: (i, 0)),
        compiler_params=pltpu.CompilerParams(
            dimension_semantics=("arbitrary",), vmem_limit_bytes=VMEM_LIMIT_BYTES),
        name="ffn" if final_gain is None else "ffn_final",
    )(*operands)


def _chunk_constants():
    idx = np.arange(CHUNK)
    def tile(m):
        return np.tile(m.astype(np.float32), (1, GROUP))
    masks = [tile(idx[None, :] < idx[:, None])]
    for l in range(1, N_LEVELS + 1):
        same = (idx[:, None] >> l) == (idx[None, :] >> l)
        cross = ((idx[:, None] >> (l - 1)) & 1 == 1) & ((idx[None, :] >> (l - 1)) & 1 == 0)
        masks.append(tile(same & cross))
    masks.append(tile(idx[None, :] <= idx[:, None]))
    masks.append(tile(idx[None, :] == idx[:, None]))
    masks = np.stack(masks)
    ltri = (idx[None, :] <= idx[:, None]).astype(np.float32)
    mats = [ltri]
    for l in range(1, N_LEVELS + 1):
        hblk = idx >> (l - 1)
        half = hblk & 1
        same_h = hblk[:, None] == hblk[None, :]
        eq = same_h & (idx[None, :] <= idx[:, None]) & (half[:, None] == 1)
        ek = same_h & (idx[None, :] > idx[:, None]) & (half[:, None] == 0)
        mats.append((eq | ek).astype(np.float32))
    emat = np.concatenate(mats, axis=0)
    return masks, np.concatenate([emat, emat], axis=1), np.concatenate([ltri, ltri], axis=1)


def _mixer_kernel(x_ref, mixg_ref, win_ref, aw2_ref, ab_ref, gnorm_ref, mu_ref, w0_ref, w2_ref,
                  a0_ref, a2_ref, g2_ref, kk_ref, ka_ref, rk_ref, lnw_ref, lnb_ref, wout_ref,
                  mask_ref, emat_ref, ltri_ref,
                  o_ref,
                  proj_ref, carry_ref, r_ref, k_ref, v_ref, lw_ref, av_ref, bv_ref, la_ref,
                  mix_ref, gst_ref, rst_ref, *, tile, tiles_per_seq):
    step = pl.program_id(0)

    @pl.when(step % tiles_per_seq == 0)
    def _():
        carry_ref[...] = jnp.zeros_like(carry_ref)
        gst_ref[...] = jnp.zeros_like(gst_ref)
        rst_ref[...] = jnp.zeros_like(rst_ref)

    x = x_ref[...]
    h = _rms(x, mixg_ref[...]).astype(BF16)
    proj_ref[...] = jnp.dot(h, win_ref[...], preferred_element_type=F32)

    z = _dot(proj_ref[:, OFF_ALR:OFF_ALR + LANES], aw2_ref[...]) + ab_ref[...]
    la_ref[...] = (jnp.minimum(z, 0.0) - _log1pexp_neg_abs(z)) * (1.0 / GLA_GATE_NORMALIZER)

    p = proj_ref[:, OFF_RW:OFF_RW + RW_COLS]
    row = lax.broadcasted_iota(jnp.int32, p.shape, 0)
    prev = jnp.where(row == 0, carry_ref[...], pltpu.roll(p, 1, axis=0))
    carry_ref[...] = p[tile - 1:tile, :]
    p = p + mu_ref[...] * (prev - p)
    r = p[:, RW_R:RW_R + RWKV_W]
    k = p[:, RW_K:RW_K + RWKV_W]
    v = p[:, RW_V:RW_V + RWKV_W]
    lr = p[:, RW_LR:RW_LR + LANES]
    u = w0_ref[...] + _dot(jnp.tanh(lr), w2_ref[...])
    w = jnp.minimum(u, 0.0) - _log1pexp_neg_abs(u) - 0.5
    a = _sigmoid(a0_ref[...] + _dot(lr, a2_ref[...]))
    kk = k * kk_ref[...]
    kk = kk / jnp.maximum(jnp.sqrt(_pair_sums(kk * kk)), 1e-12)
    k = k * (1.0 + (a - 1.0) * ka_ref[...])
    r_ref[...] = r
    k_ref[...] = k
    v_ref[...] = v
    lw_ref[...] = -jnp.exp(w)
    av_ref[...] = -kk
    bv_ref[...] = kk * a
    bonus = _pair_sums(r * k * rk_ref[...]) * v
    gate = _dot(_sigmoid(p[:, RW_G:RW_G + LANES]), g2_ref[...])

    def mask(i):
        return mask_ref[i] > 0.0

    n_chunks = tile // CHUNK
    chunk_rows = [slice(c * CHUNK, (c + 1) * CHUNK) for c in range(n_chunks)]

    gla = []
    for rows in chunk_rows:
        d = {}
        d["q"] = proj_ref[rows, OFF_Q:OFF_Q + GLA_QK] * (GLA_DK ** -0.5)
        d["k"] = proj_ref[rows, OFF_K:OFF_K + GLA_QK]
        d["v"] = proj_ref[rows, OFF_V:OFF_V + GLA_V]
        d["eg"] = _split_dot(emat_ref[...], la_ref[rows, :])
        gla.append(d)
    for d in gla:
        d["scores"] = jnp.where(mask(8), _dot_nt(d["q"], _bd_rows(d["k"], GLA_DK)), 0.0)
    for l in range(1, N_LEVELS + 1):
        for d in gla:
            e_l = jnp.exp(d["eg"][CHUNK * l:CHUNK * (l + 1)])
            d["scores"] = d["scores"] + jnp.where(mask(l), _dot_nt(d["q"] * e_l, _bd_rows(d["k"] * e_l, GLA_DK)), 0.0)
    for d in gla:
        cum = d["eg"][0:CHUNK]
        last = cum[CHUNK - 1:CHUNK, :]
        d["o"] = _dot(d["scores"], _bd_rows(d["v"], GLA_DV))
        d["qe"] = d["q"] * jnp.exp(cum)
        kd = d["k"] * jnp.exp(last - cum)
        stack_t = jnp.concatenate([kd, jnp.broadcast_to(jnp.exp(last), kd.shape)], axis=0).T
        d["decay_col"] = stack_t[:, CHUNK:CHUNK + 1]
        vpad = jnp.concatenate([d["v"], jnp.zeros_like(d["v"])], axis=0)
        kv = []
        for pr in range(GLA_HEADS // 2):
            both = _dot(stack_t[pr * 2 * GLA_DK:(pr + 1) * 2 * GLA_DK, :],
                        vpad[:, pr * 2 * GLA_DV:(pr + 1) * 2 * GLA_DV])
            kv += [both[0:GLA_DK, 0:GLA_DV], both[GLA_DK:, GLA_DV:]]
        d["kv"] = jnp.concatenate(kv, axis=0)
    gst = gst_ref[...]
    for rows, d in zip(chunk_rows, gla):
        o = d["o"] + _dot(d["qe"], _stack_to_bd(gst, GLA_DK))
        gst = gst * d["decay_col"] + d["kv"]
        for hd in range(GLA_HEADS):
            sl = slice(hd * GLA_DV, (hd + 1) * GLA_DV)
            oh = o[:, sl]
            oh = oh * lax.rsqrt(jnp.mean(oh * oh, axis=-1, keepdims=True) + NORM_EPS) * gnorm_ref[...]
            go = proj_ref[rows, OFF_GO + hd * GLA_DV:OFF_GO + (hd + 1) * GLA_DV]
            mix_ref[rows, sl] = oh * (go * _sigmoid(go))
    gst_ref[...] = gst

    n_groups = RWKV_HEADS // GROUP
    gw = GROUP * RWKV_N
    items = []
    for rows in chunk_rows:
        cum_all = _split_dot(ltri_ref[...], lw_ref[rows, :])
        for g in range(n_groups):
            gl = slice(g * gw, (g + 1) * gw)
            rc, kc, vc = r_ref[rows, gl], k_ref[rows, gl], v_ref[rows, gl]
            lw, ac, bc = lw_ref[rows, gl], av_ref[rows, gl], bv_ref[rows, gl]
            cum = cum_all[:, gl]
            last = cum[CHUNK - 1:CHUNK, :]
            e_neg = jnp.exp(-cum)
            e_last = jnp.exp(last - cum)
            d = {"rows": rows, "g": g, "v": vc, "e_last_row": jnp.exp(last)}
            d["at"] = ac * jnp.exp(cum - lw)
            d["rt"] = rc * jnp.exp(cum)
            d["bt_bd"] = _bd_rows(bc * e_neg, RWKV_N)
            d["kt_bd"] = _bd_rows(kc * e_neg, RWKV_N)
            d["bk_t"] = jnp.concatenate([bc * e_last, kc * e_last], axis=0).T
            d["vbd"] = _bd_rows(vc, RWKV_N)
            items.append(d)
    for d in items:
        ar = jnp.concatenate([d["at"], d["rt"]], axis=0)
        ab_ = _dot_nt(ar, d["bt_bd"])
        ak_ = _dot_nt(ar, d["kt_bd"])
        d["a_ab"] = jnp.where(mask(0), ab_[0:CHUNK], 0.0)
        d["a_ak"] = jnp.where(mask(0), ak_[0:CHUNK], 0.0)
        d["a_rb"] = jnp.where(mask(7), ab_[CHUNK:], 0.0)
        d["a_rk"] = jnp.where(mask(7), ak_[CHUNK:], 0.0)
        d["tinv"] = jnp.where(mask(8), 1.0, 0.0) + jnp.where(mask(1), d["a_ab"], 0.0)
    for d in items:
        d["akv"] = _dot(d["a_ak"], d["vbd"])
    for l in range(2, N_LEVELS + 1):
        for d in items:
            d["x"] = _dot(jnp.where(mask(l), d["a_ab"], 0.0), _bd_rows(d["tinv"], CHUNK))
        for d in items:
            d["tinv"] = d["tinv"] + _dot(d["tinv"], _bd_rows(d["x"], CHUNK))
    for d in items:
        tw = _dot(d["tinv"], jnp.concatenate([_bd_rows(d["at"], RWKV_N), _bd_rows(d["akv"], RWKV_N)], axis=1))
        d["ap"], d["uv"] = tw[:, 0:gw], tw[:, gw:]
    for d in items:
        d["rp"] = d["rt"] + _dot(d["a_rb"], _bd_rows(d["ap"], RWKV_N))
        d["yv"] = _dot(jnp.concatenate([d["a_rb"], d["a_rk"]], axis=1),
                       jnp.concatenate([_bd_rows(d["uv"], RWKV_N), d["vbd"]], axis=0))
    rowi = lax.broadcasted_iota(jnp.int32, (gw, gw), 0)
    coli = lax.broadcasted_iota(jnp.int32, (gw, gw), 1)
    for d in items:
        m_tiles, n_tiles = [], []
        for pr in range(GROUP // 2):
            pl_ = slice(pr * LANES, (pr + 1) * LANES)
            rhs = jnp.concatenate(
                [jnp.concatenate([d["ap"][:, pl_], d["uv"][:, pl_]], axis=1),
                 jnp.concatenate([jnp.zeros_like(d["v"][:, pl_]), d["v"][:, pl_]], axis=1)], axis=0)
            both = _dot(d["bk_t"][pr * LANES:(pr + 1) * LANES, :], rhs)
            m_tiles.append(both[:, 0:LANES])
            n_tiles.append(both[:, LANES:])
        d["mc"] = _pair_blocks_to_bd(m_tiles) + jnp.where(rowi == coli, d["e_last_row"], 0.0)
        d["nc"] = _pair_blocks_to_bd(n_tiles)
    states = [rst_ref[g] for g in range(n_groups)]
    for d in items:
        g = d["g"]
        mix_ref[d["rows"], GLA_V + g * gw:GLA_V + (g + 1) * gw] = _dot(d["rp"], states[g]) + d["yv"]
        states[g] = _dot(d["mc"], states[g]) + d["nc"]
    for g in range(n_groups):
        rst_ref[g] = states[g]

    y = mix_ref[:, GLA_V:]
    mean = _pair_sums(y) * (1.0 / RWKV_N)
    yc = y - mean
    var = _pair_sums(yc * yc) * (1.0 / RWKV_N)
    y = yc * lax.rsqrt(var + RWKV_GN_EPS) * lnw_ref[...] + lnb_ref[...]
    mix_ref[:, GLA_V:] = (y + bonus) * gate
    o_ref[...] = x + jnp.dot(mix_ref[...].astype(BF16), wout_ref[...], preferred_element_type=F32)


def _mixer(x2d, seq, mixer_params, *, tile):
    rows, d = x2d.shape
    masks, emat, ltri = _chunk_constants()
    consts = [jnp.asarray(masks), jnp.asarray(emat, BF16), jnp.asarray(ltri, BF16)]
    resident = list(mixer_params) + consts
    return pl.pallas_call(
        functools.partial(_mixer_kernel, tile=tile, tiles_per_seq=seq // tile),
        out_shape=jax.ShapeDtypeStruct((rows, d), F32),
        grid=(rows // tile,),
        in_specs=[pl.BlockSpec((tile, d), lambda s: (s, 0))] + [_const_spec(a.shape) for a in resident],
        out_specs=pl.BlockSpec((tile, d), lambda s: (s, 0)),
        scratch_shapes=[
            pltpu.VMEM((tile, PROJ_PAD), F32),
            pltpu.VMEM((1, RW_COLS), F32),
            pltpu.VMEM((tile, RWKV_W), F32),
            pltpu.VMEM((tile, RWKV_W), F32),
            pltpu.VMEM((tile, RWKV_W), F32),
            pltpu.VMEM((tile, RWKV_W), F32),
            pltpu.VMEM((tile, RWKV_W), F32),
            pltpu.VMEM((tile, RWKV_W), F32),
            pltpu.VMEM((tile, GLA_QK), F32),
            pltpu.VMEM((tile, GLA_V + RWKV_W), F32),
            pltpu.VMEM((GLA_HEADS * GLA_DK, GLA_DV), F32),
            pltpu.VMEM((RWKV_HEADS // GROUP, GROUP * RWKV_N, GROUP * RWKV_N), F32),
        ],
        compiler_params=pltpu.CompilerParams(
            dimension_semantics=("arbitrary",), vmem_limit_bytes=VMEM_LIMIT_BYTES),
        name="mixer",
    )(x2d, *resident)


def _row(vec):
    return vec.reshape(1, -1).astype(F32)


def kernel(x, ffn1_norm, ffn1_w_gate, ffn1_w_up, ffn1_w_down, mix_norm, w_in, gla_alpha_w2, gla_alpha_b, gla_norm, rwkv_mu, rwkv_w0, rwkv_w2, rwkv_a0, rwkv_a2, rwkv_g2, rwkv_k_k, rwkv_k_a, rwkv_r_k, rwkv_ln_w, rwkv_ln_b, w_out, ffn2_norm, ffn2_w_gate, ffn2_w_up, ffn2_w_down, final_norm):
    bsz, seq, d = x.shape
    depth = ffn1_norm.shape[0]
    assert depth == 1, "the final norm is fused into the last FFN call, so one layer is assumed"
    rows = bsz * seq
    row_tile = min(FFN_ROWS, rows)
    tile = min(MIX_TILE, seq)
    gla_cols = OFF_ALR + GLA_RANK
    l = 0
    x2 = _ffn(x.reshape(rows, d), _row(ffn1_norm[l]), ffn1_w_gate[l].astype(BF16),
              ffn1_w_up[l].astype(BF16), ffn1_w_down[l].astype(BF16), row_tile=row_tile)
    win = jnp.concatenate([w_in[l][:, :gla_cols], jnp.zeros((d, LANES - GLA_RANK), F32),
                           w_in[l][:, gla_cols:]], axis=1).astype(BF16)
    aw2 = jnp.concatenate([gla_alpha_w2[l], jnp.zeros((LANES - GLA_RANK, GLA_QK), F32)], axis=0).astype(BF16)
    zeros64 = jnp.zeros((64, RWKV_W), F32)
    w2 = jnp.concatenate([rwkv_w2[l], zeros64], axis=0).astype(BF16)
    a2 = jnp.concatenate([zeros64, rwkv_a2[l]], axis=0).astype(BF16)
    mixer_params = [_row(mix_norm[l]), win, aw2, _row(gla_alpha_b[l]), _row(gla_norm[l]),
                    _row(rwkv_mu[l]), _row(rwkv_w0[l]), w2, _row(rwkv_a0[l]), a2, rwkv_g2[l].astype(BF16),
                    _row(rwkv_k_k[l]), _row(rwkv_k_a[l]), _row(rwkv_r_k[l]), _row(rwkv_ln_w[l]),
                    _row(rwkv_ln_b[l]), w_out[l].astype(BF16)]
    x3 = _mixer(x2, seq, mixer_params, tile=tile)
    out = _ffn(x3, _row(ffn2_norm[l]), ffn2_w_gate[l].astype(BF16), ffn2_w_up[l].astype(BF16),
               ffn2_w_down[l].astype(BF16), _row(final_norm), row_tile=row_tile)
    return out.reshape(bsz, seq, d)
```

```python
import functools

import numpy as np
import jax
import jax.numpy as jnp
from jax import lax
from jax.experimental import pallas as pl
from jax.experimental.pallas import tpu as pltpu

F32 = jnp.float32
BF16 = jnp.bfloat16

NORM_EPS = 1e-6
MACARON_WEIGHT = 0.5
CHUNK = 64
GROUP = 4
LANES = 128
GLA_HEADS = 4
GLA_DK = 64
GLA_DV = 128
GLA_RANK = 16
GLA_GATE_NORMALIZER = 16.0
RWKV_HEADS = 8
RWKV_N = 64
RWKV_GN_EPS = 64e-5
GLA_QK = GLA_HEADS * GLA_DK
GLA_V = GLA_HEADS * GLA_DV
RWKV_W = RWKV_HEADS * RWKV_N
OFF_Q = 0
OFF_K = OFF_Q + GLA_QK
OFF_V = OFF_K + GLA_QK
OFF_GO = OFF_V + GLA_V
OFF_ALR = OFF_GO + GLA_V
OFF_RW = OFF_ALR + LANES
RW_COLS = 3 * RWKV_W + 64 + 64 + 128
RW_R, RW_K, RW_V, RW_LR, RW_G = 0, RWKV_W, 2 * RWKV_W, 3 * RWKV_W, 3 * RWKV_W + LANES
PROJ_PAD = OFF_RW + RW_COLS
N_LEVELS = 6
MIX_TILE = 512
FFN_ROWS = 1024
VMEM_LIMIT_BYTES = 56 * 1024 * 1024


def _dot(a, b):
    return jnp.dot(a.astype(BF16), b.astype(BF16), preferred_element_type=F32)


def _dot_nt(a, b):
    return lax.dot_general(a.astype(BF16), b.astype(BF16), (((1,), (1,)), ((), ())),
                           preferred_element_type=F32)


def _split_dot(m, x):
    hi = x.astype(BF16)
    lo = (x - hi.astype(F32)).astype(BF16)
    return jnp.dot(m, jnp.concatenate([hi, lo], axis=0), preferred_element_type=F32)


def _sigmoid(x):
    return 1.0 / (1.0 + jnp.exp(-x))


def _log1pexp_neg_abs(x):
    return jnp.log(1.0 + jnp.exp(-jnp.abs(x)))


def _rms(x, gain):
    return x * lax.rsqrt(jnp.mean(x * x, axis=-1, keepdims=True) + NORM_EPS) * gain


def _bd_rows(y, width):
    n_tiles = y.shape[1] // LANES
    lane = lax.broadcasted_iota(jnp.int32, (y.shape[0], LANES), 1)
    rows = []
    for h in range(GROUP):
        lo, hi = h * width, (h + 1) * width
        tiles = []
        for j in range(n_tiles):
            tl, th = j * LANES, (j + 1) * LANES
            tile = y[:, tl:th]
            if hi <= tl or lo >= th:
                tiles.append(jnp.zeros_like(tile))
            elif lo <= tl and hi >= th:
                tiles.append(tile)
            elif lo <= tl:
                tiles.append(jnp.where(lane < hi - tl, tile, 0.0))
            else:
                tiles.append(jnp.where(lane >= lo - tl, tile, 0.0))
        rows.append(jnp.concatenate(tiles, axis=1) if n_tiles > 1 else tiles[0])
    return jnp.concatenate(rows, axis=0)


def _stack_to_bd(s, rows_per_head):
    rows = []
    for h in range(GROUP):
        blk = s[h * rows_per_head:(h + 1) * rows_per_head, :]
        rows.append(jnp.concatenate([blk if j == h else jnp.zeros_like(blk) for j in range(GROUP)], axis=1))
    return jnp.concatenate(rows, axis=0)


def _pair_blocks_to_bd(tiles):
    half = tiles[0].shape[0] // 2
    lane = lax.broadcasted_iota(jnp.int32, (half, LANES), 1)
    rows = []
    for p, t in enumerate(tiles):
        kept = jnp.concatenate([jnp.where(lane < half, t[0:half], 0.0),
                                jnp.where(lane >= half, t[half:], 0.0)], axis=0)
        rows.append(jnp.concatenate([kept if j == p else jnp.zeros_like(kept) for j in range(len(tiles))], axis=1))
    return jnp.concatenate(rows, axis=0)


def _pair_sums(x):
    lane = lax.broadcasted_iota(jnp.int32, (x.shape[0], LANES), 1)
    first = lane < RWKV_N
    outs = []
    for p in range(x.shape[1] // LANES):
        xp = x[:, p * LANES:(p + 1) * LANES]
        s0 = jnp.sum(jnp.where(first, xp, 0.0), axis=-1, keepdims=True)
        s1 = jnp.sum(jnp.where(first, 0.0, xp), axis=-1, keepdims=True)
        outs.append(jnp.where(first, s0, s1))
    return jnp.concatenate(outs, axis=1)


def _ffn_half_step(x, gain_ref, wg_ref, wu_ref, wd_ref):
    h = _rms(x, gain_ref[...]).astype(BF16)
    gate = jnp.dot(h, wg_ref[...], preferred_element_type=F32)
    up = jnp.dot(h, wu_ref[...], preferred_element_type=F32)
    act = (gate * _sigmoid(gate) * up).astype(BF16)
    return x + MACARON_WEIGHT * jnp.dot(act, wd_ref[...], preferred_element_type=F32)


def _ffn_kernel(x_ref, gain_ref, wg_ref, wu_ref, wd_ref, o_ref):
    o_ref[...] = _ffn_half_step(x_ref[...], gain_ref, wg_ref, wu_ref, wd_ref)


def _ffn_final_kernel(x_ref, gain_ref, wg_ref, wu_ref, wd_ref, fgain_ref, o_ref):
    o_ref[...] = _rms(_ffn_half_step(x_ref[...], gain_ref, wg_ref, wu_ref, wd_ref), fgain_ref[...])


def _const_spec(shape):
    return pl.BlockSpec(shape, lambda *_: (0,) * len(shape), pipeline_mode=pl.Buffered(1))


def _ffn(x2d, gain, wg, wu, wd, final_gain=None, *, row_tile):
    rows, d = x2d.shape
    operands = [x2d, gain, wg, wu, wd] + ([] if final_gain is None else [final_gain])
    return pl.pallas_call(
        _ffn_kernel if final_gain is None else _ffn_final_kernel,
        out_shape=jax.ShapeDtypeStruct((rows, d), F32),
        grid=(rows // row_tile,),
        in_specs=[pl.BlockSpec((row_tile, d), lambda i: (i, 0))] + [_const_spec(a.shape) for a in operands[1:]],
        out_specs=pl.BlockSpec((row_tile, d), lambda i: (i, 0)),
        compiler_params=pltpu.CompilerParams(
            dimension_semantics=("arbitrary",), vmem_limit_bytes=VMEM_LIMIT_BYTES),
        name="ffn" if final_gain is None else "ffn_final",
    )(*operands)


def _chunk_constants():
    idx = np.arange(CHUNK)
    def tile(m):
        return np.tile(m.astype(np.float32), (1, GROUP))
    masks = [tile(idx[None, :] < idx[:, None])]
    for l in range(1, N_LEVELS + 1):
        same = (idx[:, None] >> l) == (idx[None, :] >> l)
        cross = ((idx[:, None] >> (l - 1)) & 1 == 1) & ((idx[None, :] >> (l - 1)) & 1 == 0)
        masks.append(tile(same & cross))
    masks.append(tile(idx[None, :] <= idx[:, None]))
    masks.append(tile(idx[None, :] == idx[:, None]))
    masks = np.stack(masks)
    ltri = (idx[None, :] <= idx[:, None]).astype(np.float32)
    mats = [ltri]
    for l in range(1, N_LEVELS + 1):
        hblk = idx >> (l - 1)
        half = hblk & 1
        same_h = hblk[:, None] == hblk[None, :]
        eq = same_h & (idx[None, :] <= idx[:, None]) & (half[:, None] == 1)
        ek = same_h & (idx[None, :] > idx[:, None]) & (half[:, None] == 0)
        mats.append((eq | ek).astype(np.float32))
    emat = np.concatenate(mats, axis=0)
    return masks, np.concatenate([emat, emat], axis=1), np.concatenate([ltri, ltri], axis=1)


def _mixer_kernel(x_ref, mixg_ref, win_ref, aw2_ref, ab_ref, gnorm_ref, mu_ref, w0_ref, w2_ref,
                  a0_ref, a2_ref, g2_ref, kk_ref, ka_ref, rk_ref, lnw_ref, lnb_ref, wout_ref,
                  mask_ref, emat_ref, ltri_ref,
                  o_ref,
                  proj_ref, carry_ref, r_ref, k_ref, v_ref, lw_ref, av_ref, bv_ref, la_ref,
                  mix_ref, gst_ref, rst_ref, *, tile, tiles_per_seq):
    step = pl.program_id(0)

    @pl.when(step % tiles_per_seq == 0)
    def _():
        carry_ref[...] = jnp.zeros_like(carry_ref)
        gst_ref[...] = jnp.zeros_like(gst_ref)
        rst_ref[...] = jnp.zeros_like(rst_ref)

    x = x_ref[...]
    h = _rms(x, mixg_ref[...]).astype(BF16)
    proj_ref[...] = jnp.dot(h, win_ref[...], preferred_element_type=F32)

    def mask(i):
        return mask_ref[i] > 0.0

    n_groups = RWKV_HEADS // GROUP
    gw = GROUP * RWKV_N
    rowi = lax.broadcasted_iota(jnp.int32, (gw, gw), 0)
    coli = lax.broadcasted_iota(jnp.int32, (gw, gw), 1)

    def front(h0, rows_n, out):
        hrows = slice(h0, h0 + rows_n)
        p = proj_ref[hrows, OFF_RW:OFF_RW + RW_COLS]
        before = carry_ref[...] if h0 == 0 else proj_ref[h0 - 1:h0, OFF_RW:OFF_RW + RW_COLS]
        row = lax.broadcasted_iota(jnp.int32, p.shape, 0)
        prev = jnp.where(row == 0, before, pltpu.roll(p, 1, axis=0))
        p = p + mu_ref[...] * (prev - p)
        yield
        r = p[:, RW_R:RW_R + RWKV_W]
        k = p[:, RW_K:RW_K + RWKV_W]
        v = p[:, RW_V:RW_V + RWKV_W]
        lr = p[:, RW_LR:RW_LR + LANES]
        u = w0_ref[...] + _dot(jnp.tanh(lr), w2_ref[...])
        w = jnp.minimum(u, 0.0) - _log1pexp_neg_abs(u) - 0.5
        yield
        a = _sigmoid(a0_ref[...] + _dot(lr, a2_ref[...]))
        kk = k * kk_ref[...]
        kk = kk / jnp.maximum(jnp.sqrt(_pair_sums(kk * kk)), 1e-12)
        yield
        k = k * (1.0 + (a - 1.0) * ka_ref[...])
        r_ref[hrows, :] = r
        k_ref[hrows, :] = k
        v_ref[hrows, :] = v
        lw_ref[hrows, :] = -jnp.exp(w)
        av_ref[hrows, :] = -kk
        bv_ref[hrows, :] = kk * a
        yield
        out["bonus"] = _pair_sums(r * k * rk_ref[...]) * v
        out["gate"] = _dot(_sigmoid(p[:, RW_G:RW_G + LANES]), g2_ref[...])
        z = _dot(proj_ref[hrows, OFF_ALR:OFF_ALR + LANES], aw2_ref[...]) + ab_ref[...]
        la_ref[hrows, :] = (jnp.minimum(z, 0.0) - _log1pexp_neg_abs(z)) * (1.0 / GLA_GATE_NORMALIZER)
        yield

        chunk_rows = [slice(h0 + c * CHUNK, h0 + (c + 1) * CHUNK) for c in range(rows_n // CHUNK)]
        gla = []
        for rows in chunk_rows:
            d = {"rows": rows}
            d["q"] = proj_ref[rows, OFF_Q:OFF_Q + GLA_QK] * (GLA_DK ** -0.5)
            d["k"] = proj_ref[rows, OFF_K:OFF_K + GLA_QK]
            d["v"] = proj_ref[rows, OFF_V:OFF_V + GLA_V]
            d["eg"] = _split_dot(emat_ref[...], la_ref[rows, :])
            gla.append(d)
            yield
        for d in gla:
            d["scores"] = jnp.where(mask(8), _dot_nt(d["q"], _bd_rows(d["k"], GLA_DK)), 0.0)
        yield
        for l in range(1, N_LEVELS + 1):
            for d in gla:
                e_l = jnp.exp(d["eg"][CHUNK * l:CHUNK * (l + 1)])
                d["scores"] = d["scores"] + jnp.where(
                    mask(l), _dot_nt(d["q"] * e_l, _bd_rows(d["k"] * e_l, GLA_DK)), 0.0)
                yield
        for d in gla:
            cum = d["eg"][0:CHUNK]
            last = cum[CHUNK - 1:CHUNK, :]
            d["o"] = _dot(d["scores"], _bd_rows(d["v"], GLA_DV))
            d["qe"] = d["q"] * jnp.exp(cum)
            kd = d["k"] * jnp.exp(last - cum)
            stack_t = jnp.concatenate([kd, jnp.broadcast_to(jnp.exp(last), kd.shape)], axis=0).T
            d["decay_col"] = stack_t[:, CHUNK:CHUNK + 1]
            vpad = jnp.concatenate([d["v"], jnp.zeros_like(d["v"])], axis=0)
            kv = []
            for pr in range(GLA_HEADS // 2):
                both = _dot(stack_t[pr * 2 * GLA_DK:(pr + 1) * 2 * GLA_DK, :],
                            vpad[:, pr * 2 * GLA_DV:(pr + 1) * 2 * GLA_DV])
                kv += [both[0:GLA_DK, 0:GLA_DV], both[GLA_DK:, GLA_DV:]]
            d["kv"] = jnp.concatenate(kv, axis=0)
            yield
        out["gla"] = gla

        items = []
        for rows in chunk_rows:
            cum_all = _split_dot(ltri_ref[...], lw_ref[rows, :])
            for g in range(n_groups):
                gl = slice(g * gw, (g + 1) * gw)
                rc, kc, vc = r_ref[rows, gl], k_ref[rows, gl], v_ref[rows, gl]
                lw, ac, bc = lw_ref[rows, gl], av_ref[rows, gl], bv_ref[rows, gl]
                cum = cum_all[:, gl]
                last = cum[CHUNK - 1:CHUNK, :]
                e_neg = jnp.exp(-cum)
                e_last = jnp.exp(last - cum)
                d = {"rows": rows, "g": g, "v": vc, "e_last_row": jnp.exp(last)}
                d["at"] = ac * jnp.exp(cum - lw)
                d["rt"] = rc * jnp.exp(cum)
                d["bt_bd"] = _bd_rows(bc * e_neg, RWKV_N)
                d["kt_bd"] = _bd_rows(kc * e_neg, RWKV_N)
                d["bk_t"] = jnp.concatenate([bc * e_last, kc * e_last], axis=0).T
                d["vbd"] = _bd_rows(vc, RWKV_N)
                items.append(d)
            yield
        out["items"] = items

    def rwkv_products(items, tick):
        for d in items:
            ar = jnp.concatenate([d["at"], d["rt"]], axis=0)
            ab_ = _dot_nt(ar, d["bt_bd"])
            ak_ = _dot_nt(ar, d["kt_bd"])
            d["a_ab"] = jnp.where(mask(0), ab_[0:CHUNK], 0.0)
            d["a_ak"] = jnp.where(mask(0), ak_[0:CHUNK], 0.0)
            d["a_rb"] = jnp.where(mask(7), ab_[CHUNK:], 0.0)
            d["a_rk"] = jnp.where(mask(7), ak_[CHUNK:], 0.0)
            d["tinv"] = jnp.where(mask(8), 1.0, 0.0) + jnp.where(mask(1), d["a_ab"], 0.0)
        tick()
        for d in items:
            d["akv"] = _dot(d["a_ak"], d["vbd"])
        tick()
        for l in range(2, N_LEVELS + 1):
            for d in items:
                d["x"] = _dot(jnp.where(mask(l), d["a_ab"], 0.0), _bd_rows(d["tinv"], CHUNK))
            tick()
            for d in items:
                d["tinv"] = d["tinv"] + _dot(d["tinv"], _bd_rows(d["x"], CHUNK))
            tick()
        for d in items:
            tw = _dot(d["tinv"], jnp.concatenate([_bd_rows(d["at"], RWKV_N), _bd_rows(d["akv"], RWKV_N)], axis=1))
            d["ap"], d["uv"] = tw[:, 0:gw], tw[:, gw:]
        tick()
        for d in items:
            d["rp"] = d["rt"] + _dot(d["a_rb"], _bd_rows(d["ap"], RWKV_N))
            d["yv"] = _dot(jnp.concatenate([d["a_rb"], d["a_rk"]], axis=1),
                           jnp.concatenate([_bd_rows(d["uv"], RWKV_N), d["vbd"]], axis=0))
        tick()
        for d in items:
            m_tiles, n_tiles = [], []
            for pr in range(GROUP // 2):
                pl_ = slice(pr * LANES, (pr + 1) * LANES)
                rhs = jnp.concatenate(
                    [jnp.concatenate([d["ap"][:, pl_], d["uv"][:, pl_]], axis=1),
                     jnp.concatenate([jnp.zeros_like(d["v"][:, pl_]), d["v"][:, pl_]], axis=1)], axis=0)
                both = _dot(d["bk_t"][pr * LANES:(pr + 1) * LANES, :], rhs)
                m_tiles.append(both[:, 0:LANES])
                n_tiles.append(both[:, LANES:])
            d["mc"] = _pair_blocks_to_bd(m_tiles) + jnp.where(rowi == coli, d["e_last_row"], 0.0)
            d["nc"] = _pair_blocks_to_bd(n_tiles)
        tick()

    carried = {"gst": gst_ref[...],
               "rwkv": [rst_ref[g] for g in range(n_groups)]}

    def tail(h0, rows_n, out):
        hrows = slice(h0, h0 + rows_n)
        for d in out["gla"]:
            o = d["o"] + _dot(d["qe"], _stack_to_bd(carried["gst"], GLA_DK))
            carried["gst"] = carried["gst"] * d["decay_col"] + d["kv"]
            for hd in range(GLA_HEADS):
                sl = slice(hd * GLA_DV, (hd + 1) * GLA_DV)
                oh = o[:, sl]
                oh = oh * lax.rsqrt(jnp.mean(oh * oh, axis=-1, keepdims=True) + NORM_EPS) * gnorm_ref[...]
                go = proj_ref[d["rows"], OFF_GO + hd * GLA_DV:OFF_GO + (hd + 1) * GLA_DV]
                mix_ref[d["rows"], sl] = oh * (go * _sigmoid(go))
            yield
        for d in out["items"]:
            g = d["g"]
            mix_ref[d["rows"], GLA_V + g * gw:GLA_V + (g + 1) * gw] = _dot(d["rp"], carried["rwkv"][g]) + d["yv"]
            carried["rwkv"][g] = _dot(d["mc"], carried["rwkv"][g]) + d["nc"]
            yield
        y = mix_ref[hrows, GLA_V:]
        mean = _pair_sums(y) * (1.0 / RWKV_N)
        yc = y - mean
        var = _pair_sums(yc * yc) * (1.0 / RWKV_N)
        y = yc * lax.rsqrt(var + RWKV_GN_EPS) * lnw_ref[...] + lnb_ref[...]
        mix_ref[hrows, GLA_V:] = (y + out["bonus"]) * out["gate"]
        yield

    def advance(gen, n=1):
        for _ in range(n):
            next(gen, None)

    def drain(gen):
        for _ in gen:
            pass

    halves = [(0, tile // 2), (tile // 2, tile // 2)] if tile >= 2 * CHUNK else [(0, tile)]
    outs = [dict() for _ in halves]
    drain(front(*halves[0], outs[0]))
    for i, half in enumerate(halves):
        nxt = front(*halves[i + 1], outs[i + 1]) if i + 1 < len(halves) else iter(())
        prv = tail(*halves[i - 1], outs[i - 1]) if i > 0 else iter(())
        rwkv_products(outs[i]["items"], lambda: (advance(prv, 1), advance(nxt, 4)))
        drain(prv)
        drain(nxt)
    drain(tail(*halves[-1], outs[-1]))

    gst_ref[...] = carried["gst"]
    for g in range(n_groups):
        rst_ref[g] = carried["rwkv"][g]
    carry_ref[...] = proj_ref[tile - 1:tile, OFF_RW:OFF_RW + RW_COLS]
    o_ref[...] = x + jnp.dot(mix_ref[...].astype(BF16), wout_ref[...], preferred_element_type=F32)


def _mixer(x2d, seq, mixer_params, *, tile):
    rows, d = x2d.shape
    masks, emat, ltri = _chunk_constants()
    consts = [jnp.asarray(masks), jnp.asarray(emat, BF16), jnp.asarray(ltri, BF16)]
    resident = list(mixer_params) + consts
    return pl.pallas_call(
        functools.partial(_mixer_kernel, tile=tile, tiles_per_seq=seq // tile),
        out_shape=jax.ShapeDtypeStruct((rows, d), F32),
        grid=(rows // tile,),
        in_specs=[pl.BlockSpec((tile, d), lambda s: (s, 0))] + [_const_spec(a.shape) for a in resident],
        out_specs=pl.BlockSpec((tile, d), lambda s: (s, 0)),
        scratch_shapes=[
            pltpu.VMEM((tile, PROJ_PAD), F32),
            pltpu.VMEM((1, RW_COLS), F32),
            pltpu.VMEM((tile, RWKV_W), F32),
            pltpu.VMEM((tile, RWKV_W), F32),
            pltpu.VMEM((tile, RWKV_W), F32),
            pltpu.VMEM((tile, RWKV_W), F32),
            pltpu.VMEM((tile, RWKV_W), F32),
            pltpu.VMEM((tile, RWKV_W), F32),
            pltpu.VMEM((tile, GLA_QK), F32),
            pltpu.VMEM((tile, GLA_V + RWKV_W), F32),
            pltpu.VMEM((GLA_HEADS * GLA_DK, GLA_DV), F32),
            pltpu.VMEM((RWKV_HEADS // GROUP, GROUP * RWKV_N, GROUP * RWKV_N), F32),
        ],
        compiler_params=pltpu.CompilerParams(
            dimension_semantics=("arbitrary",), vmem_limit_bytes=VMEM_LIMIT_BYTES),
        name="mixer",
    )(x2d, *resident)


def _row(vec):
    return vec.reshape(1, -1).astype(F32)


def kernel(x, ffn1_norm, ffn1_w_gate, ffn1_w_up, ffn1_w_down, mix_norm, w_in, gla_alpha_w2, gla_alpha_b, gla_norm, rwkv_mu, rwkv_w0, rwkv_w2, rwkv_a0, rwkv_a2, rwkv_g2, rwkv_k_k, rwkv_k_a, rwkv_r_k, rwkv_ln_w, rwkv_ln_b, w_out, ffn2_norm, ffn2_w_gate, ffn2_w_up, ffn2_w_down, final_norm):
    bsz, seq, d = x.shape
    depth = ffn1_norm.shape[0]
    assert depth == 1, "the final norm is fused into the last FFN call, so one layer is assumed"
    rows = bsz * seq
    row_tile = min(FFN_ROWS, rows)
    tile = min(MIX_TILE, seq)
    gla_cols = OFF_ALR + GLA_RANK
    l = 0
    x2 = _ffn(x.reshape(rows, d), _row(ffn1_norm[l]), ffn1_w_gate[l].astype(BF16),
              ffn1_w_up[l].astype(BF16), ffn1_w_down[l].astype(BF16), row_tile=row_tile)
    win = jnp.concatenate([w_in[l][:, :gla_cols], jnp.zeros((d, LANES - GLA_RANK), F32),
                           w_in[l][:, gla_cols:]], axis=1).astype(BF16)
    aw2 = jnp.concatenate([gla_alpha_w2[l], jnp.zeros((LANES - GLA_RANK, GLA_QK), F32)], axis=0).astype(BF16)
    zeros64 = jnp.zeros((64, RWKV_W), F32)
    w2 = jnp.concatenate([rwkv_w2[l], zeros64], axis=0).astype(BF16)
    a2 = jnp.concatenate([zeros64, rwkv_a2[l]], axis=0).astype(BF16)
    mixer_params = [_row(mix_norm[l]), win, aw2, _row(gla_alpha_b[l]), _row(gla_norm[l]),
                    _row(rwkv_mu[l]), _row(rwkv_w0[l]), w2, _row(rwkv_a0[l]), a2, rwkv_g2[l].astype(BF16),
                    _row(rwkv_k_k[l]), _row(rwkv_k_a[l]), _row(rwkv_r_k[l]), _row(rwkv_ln_w[l]),
                    _row(rwkv_ln_b[l]), w_out[l].astype(BF16)]
    x3 = _mixer(x2, seq, mixer_params, tile=tile)
    out = _ffn(x3, _row(ffn2_norm[l]), ffn2_w_gate[l].astype(BF16), ffn2_w_up[l].astype(BF16),
               ffn2_w_down[l].astype(BF16), _row(final_norm), row_tile=row_tile)
    return out.reshape(bsz, seq, d)
```
